```python
import math
import jax, jax.numpy as jnp
from jax import lax
import numpy as np

D_MODEL = 1024
BATCH = 4
SEQ = 4096
DEPTH = 2

GRID_W = 64
N_HEADS = 16
N_KV_HEADS = 4
HEAD_DIM = D_MODEL // N_HEADS
GROUP = N_HEADS // N_KV_HEADS
Q_BLOCK = 128
ROPE_THETA = 10000.0
ATTN_QKV = N_HEADS * HEAD_DIM + 2 * N_KV_HEADS * HEAD_DIM
GLA_HEADS = 4
GLA_DK = D_MODEL // 2 // GLA_HEADS
GLA_DV = D_MODEL // GLA_HEADS
GLA_QK = GLA_HEADS * GLA_DK
GLA_V = GLA_HEADS * GLA_DV
GATE_RANK = 16
GATE_TAU = 16.0
GLA_CHUNK = 64
GLA_IN = 2 * GLA_QK + 2 * GLA_V + 2 * GATE_RANK
D_FF = 4 * D_MODEL
N_MIXERS = 2
N_ATTN = (DEPTH + 1) // 2
N_GLA = DEPTH // 2
EPS = 1e-6

kernel_name = 'hybrid_gqa_axialrope_bigla_sqrelu'


def rmsnorm(x, g):
    x32 = x.astype(jnp.float32)
    y = x32 * lax.rsqrt(jnp.mean(x32 * x32, axis=-1, keepdims=True) + EPS)
    return (y * g.astype(jnp.float32)).astype(x.dtype)


def rope_angles(pos, dim):
    inv = ROPE_THETA ** (-jnp.arange(0, dim, 2, dtype=jnp.float32) / dim)
    ang = pos.astype(jnp.float32)[:, None] * inv[None, :]
    return jnp.cos(ang), jnp.sin(ang)


def apply_rot(x, cos, sin):
    half = x.shape[-1] // 2
    x32 = x.astype(jnp.float32)
    x1, x2 = x32[..., :half], x32[..., half:]
    c = cos[None, :, None, :]
    s = sin[None, :, None, :]
    return jnp.concatenate([x1 * c - x2 * s, x2 * c + x1 * s], axis=-1).astype(x.dtype)


def axial_rope(x, rope):
    cr, sr, cc, sc = rope
    half = HEAD_DIM // 2
    return jnp.concatenate([apply_rot(x[..., :half], cr, sr), apply_rot(x[..., half:], cc, sc)], axis=-1)


def attention_mixer(h, w_qkv, q_g, k_g, w_o, rope):
    B, S, _ = h.shape
    qkv = h @ w_qkv
    q, k, v = jnp.split(qkv, [N_HEADS * HEAD_DIM, (N_HEADS + N_KV_HEADS) * HEAD_DIM], axis=-1)
    q = rmsnorm(q.reshape(B, S, N_HEADS, HEAD_DIM), q_g)
    k = rmsnorm(k.reshape(B, S, N_KV_HEADS, HEAD_DIM), k_g)
    v = v.reshape(B, S, N_KV_HEADS, HEAD_DIM)
    q = axial_rope(q, rope)
    k = axial_rope(k, rope)
    n_blk = S // Q_BLOCK
    qb = jnp.moveaxis(q.reshape(B, n_blk, Q_BLOCK, N_KV_HEADS, GROUP, HEAD_DIM), 1, 0)
    scale = 1.0 / math.sqrt(HEAD_DIM)

    def block(qblk):
        s = jnp.einsum('bqhgd,bkhd->bhgqk', qblk, k, preferred_element_type=jnp.float32) * scale
        p = jax.nn.softmax(s, axis=-1).astype(v.dtype)
        return jnp.einsum('bhgqk,bkhd->bqhgd', p, v)

    o = lax.map(block, qb)
    o = jnp.moveaxis(o, 0, 1).reshape(B, S, N_HEADS * HEAD_DIM)
    return o @ w_o


def gla_chunked(q, k, v, logg):
    B, S, H, DK = q.shape
    DV = v.shape[-1]
    N = S // GLA_CHUNK
    to_c = lambda t: t.reshape(B, N, GLA_CHUNK, H, t.shape[-1]).transpose(0, 3, 1, 2, 4)
    q, k, v, logg = to_c(q), to_c(k), to_c(v), to_c(logg)
    b = jnp.cumsum(logg, axis=3)
    qe = q * jnp.exp(b)
    ke = k * jnp.exp(-b)
    mask = jnp.tril(jnp.ones((GLA_CHUNK, GLA_CHUNK), dtype=bool))
    a = jnp.einsum('bhncd,bhnjd->bhncj', qe, ke)
    a = jnp.where(mask, a, 0.0)
    o_intra = jnp.einsum('bhncj,bhnjv->bhncv', a, v)
    b_last = b[..., -1:, :]
    kd = k * jnp.exp(b_last - b)
    chunk_kv = jnp.einsum('bhncd,bhncv->bhndv', kd, v)
    decay = jnp.exp(b_last[..., 0, :])

    def step(state, inp):
        dec, ckv = inp
        return dec[..., None] * state + ckv, state

    init = jnp.zeros((B, H, DK, DV), jnp.float32)
    _, s_before = lax.scan(step, init, (jnp.moveaxis(decay, 2, 0), jnp.moveaxis(chunk_kv, 2, 0)))
    s_before = jnp.moveaxis(s_before, 0, 2)
    o = o_intra + jnp.einsum('bhncd,bhndv->bhncv', qe, s_before)
    return o.transpose(0, 2, 3, 1, 4).reshape(B, S, H, DV)


def gla_mixer(h, w_in, w_gate_up, b_gate, out_g, w_o):
    B, S, _ = h.shape
    f32 = jnp.float32
    proj = h @ w_in
    q, k, v, r, z = jnp.split(proj, [GLA_QK, 2 * GLA_QK, 2 * GLA_QK + GLA_V, 2 * GLA_QK + 2 * GLA_V], axis=-1)
    q = q.astype(f32).reshape(B, S, GLA_HEADS, GLA_DK) * (GLA_DK ** -0.5)
    k = k.astype(f32).reshape(B, S, GLA_HEADS, GLA_DK)
    v = v.astype(f32).reshape(B, S, GLA_HEADS, GLA_DV)
    z = z.astype(f32).reshape(B, S, 2, GATE_RANK)
    logit = jnp.einsum('bsir,ire->bsie', z, w_gate_up.astype(f32)) + b_gate.astype(f32)
    logg = (jax.nn.log_sigmoid(logit) / GATE_TAU).reshape(B, S, 2, GLA_HEADS, GLA_DK)
    o_f = gla_chunked(q, k, v, logg[:, :, 0])
    flip = lambda t: jnp.flip(t, axis=1)
    o_b = flip(gla_chunked(flip(q), flip(k), flip(v), flip(logg[:, :, 1])))
    o = rmsnorm(o_f + o_b, out_g).reshape(B, S, GLA_V)
    o = o * jax.nn.silu(r.astype(f32))
    return o.astype(h.dtype) @ w_o


def sq_relu_mlp(h, w_in, w_out):
    return jnp.square(jax.nn.relu(h @ w_in)) @ w_out


def setup_inputs(seed: int = 0) -> dict:
    key = jax.random.key(seed)
    ks = jax.random.split(key, 16)
    nrm = lambda k, shape, fan_in: jax.random.normal(k, shape, jnp.float32) * (fan_in ** -0.5)
    gain = lambda k, shape: 1.0 + 0.05 * jax.random.normal(k, shape, jnp.float32)
    return {
        'x': jax.random.normal(ks[0], (BATCH, SEQ, D_MODEL), jnp.float32),
        'norm_mix': gain(ks[1], (DEPTH, D_MODEL)),
        'norm_mlp': gain(ks[2], (DEPTH, D_MODEL)),
        'attn_w_qkv': nrm(ks[3], (N_ATTN, D_MODEL, ATTN_QKV), D_MODEL),
        'attn_q_norm': gain(ks[4], (N_ATTN, HEAD_DIM)),
        'attn_k_norm': gain(ks[5], (N_ATTN, HEAD_DIM)),
        'attn_w_o': nrm(ks[6], (N_ATTN, N_HEADS * HEAD_DIM, D_MODEL), N_HEADS * HEAD_DIM),
        'gla_w_in': nrm(ks[7], (N_GLA, D_MODEL, GLA_IN), D_MODEL),
        'gla_w_gate_up': nrm(ks[8], (N_GLA, 2, GATE_RANK, GLA_QK), GATE_RANK),
        'gla_b_gate': 0.1 * jax.random.normal(ks[9], (N_GLA, 2, GLA_QK), jnp.float32),
        'gla_out_norm': gain(ks[10], (N_GLA, GLA_DV)),
        'gla_w_o': nrm(ks[11], (N_GLA, GLA_V, D_MODEL), GLA_V),
        'mlp_w_in': nrm(ks[12], (DEPTH, D_MODEL, D_FF), D_MODEL),
        'mlp_w_out': nrm(ks[13], (DEPTH, D_FF, D_MODEL), D_FF),
        'final_norm': gain(ks[14], (D_MODEL,)),
    }


def reference(x, norm_mix, norm_mlp, attn_w_qkv, attn_q_norm, attn_k_norm, attn_w_o,
              gla_w_in, gla_w_gate_up, gla_b_gate, gla_out_norm, gla_w_o,
              mlp_w_in, mlp_w_out, final_norm):
    S = x.shape[1]
    rows = S // GRID_W
    row = jnp.repeat(jnp.arange(rows, dtype=jnp.int32), GRID_W)
    col = jnp.tile(jnp.arange(GRID_W, dtype=jnp.int32), rows)
    cr, sr = rope_angles(row, HEAD_DIM // 2)
    cc, sc = rope_angles(col, HEAD_DIM // 2)
    rope = (cr, sr, cc, sc)
    h = x
    for i in range(DEPTH):
        j = i // N_MIXERS
        hn = rmsnorm(h, norm_mix[i])
        if i % N_MIXERS == 0:
            h = h + attention_mixer(hn, attn_w_qkv[j], attn_q_norm[j], attn_k_norm[j], attn_w_o[j], rope)
        else:
            h = h + gla_mixer(hn, gla_w_in[j], gla_w_gate_up[j], gla_b_gate[j], gla_out_norm[j], gla_w_o[j])
        h = h + sq_relu_mlp(rmsnorm(h, norm_mlp[i]), mlp_w_in[i], mlp_w_out[i])
    return rmsnorm(h, final_norm)
```

```python
import functools
import math

import jax
import jax.numpy as jnp
from jax import lax
from jax.experimental import pallas as pl
from jax.experimental.pallas import tpu as pltpu

F32 = jnp.float32
BF16 = jnp.bfloat16

D_MODEL = 1024
GRID_W = 64
N_HEADS = 16
N_KV_HEADS = 4
HEAD_DIM = 64
GROUP = N_HEADS // N_KV_HEADS
ROPE_THETA = 10000.0
GLA_HEADS = 4
GLA_DK = 128
GLA_DV = 256
GLA_QK = GLA_HEADS * GLA_DK
GLA_V = GLA_HEADS * GLA_DV
GATE_RANK = 16
GATE_TAU = 16.0
D_FF = 4 * D_MODEL
EPS = 1e-6

LANES = 128
VMEM_LIMIT_BYTES = 56 * 1024 * 1024

ROW_TILE = 512
ATTN_TQ = 128
ATTN_TK = 512
FF_TILE = 1024
GLA_CHUNK = 64
GLA_PRE = 256
LOG2E = 1.4426950408889634


def _const_spec(shape):
    nd = len(shape)
    return pl.BlockSpec(shape, lambda *_: (0,) * nd, pipeline_mode=pl.Buffered(1))


def _rms(x, gain):
    ms = jnp.mean(x * x, axis=-1, keepdims=True)
    return x * lax.rsqrt(ms + EPS) * gain


def _split_dot(a_bf16, x_f32):
    hi = x_f32.astype(BF16)
    lo = (x_f32 - hi.astype(F32)).astype(BF16)
    return (jnp.dot(a_bf16, hi, preferred_element_type=F32)
            + jnp.dot(a_bf16, lo, preferred_element_type=F32))


def _qkv_kernel(x_ref, g_ref, w_ref, qg_ref, kg_ref, cos_ref, sin_ref, gsum_ref,
                q_ref, kt_ref, v_ref):
    tm = x_ref.shape[0]
    xn = _rms(x_ref[...], g_ref[...]).astype(BF16)
    y = jnp.dot(xn, w_ref[...], preferred_element_type=F32)
    cos = cos_ref[...]
    sin = sin_ref[...]
    lane = lax.broadcasted_iota(jnp.int32, (tm, LANES), 1)
    first_half = (lane % 32) < 16
    gsum = gsum_ref[...]

    def norm_rope(c, gain):
        ss = jnp.dot((c * c).astype(BF16), gsum, preferred_element_type=F32)
        cn = c * lax.rsqrt(ss * (1.0 / HEAD_DIM) + EPS) * gain
        rot = jnp.where(first_half, pltpu.roll(cn, LANES - 16, 1), pltpu.roll(cn, 16, 1))
        return cn * cos + rot * sin

    nq = N_HEADS * HEAD_DIM
    for j in range(nq // LANES):
        sl = slice(j * LANES, (j + 1) * LANES)
        q_ref[:, sl] = norm_rope(y[:, sl], qg_ref[...]).astype(BF16)
    for j in range(N_KV_HEADS * HEAD_DIM // LANES):
        kc = norm_rope(y[:, nq + j * LANES: nq + (j + 1) * LANES], kg_ref[...])
        kt = kc.T
        kt_ref[0, 2 * j] = kt[:HEAD_DIM].astype(BF16)
        kt_ref[0, 2 * j + 1] = kt[HEAD_DIM:].astype(BF16)
    v_ref[...] = y[:, nq + N_KV_HEADS * HEAD_DIM:].astype(BF16)


def _qkv_proj(x2, g, w, qg, kg, cos, sin, gsum, batch, seq):
    t = x2.shape[0]
    tm = ROW_TILE
    spb = seq // tm
    nkv = N_KV_HEADS * HEAD_DIM
    return pl.pallas_call(
        _qkv_kernel,
        grid=(t // tm,),
        in_specs=[
            pl.BlockSpec((tm, D_MODEL), lambda i: (i, 0)),
            _const_spec((1, D_MODEL)),
            _const_spec(w.shape),
            _const_spec((1, LANES)),
            _const_spec((1, LANES)),
            pl.BlockSpec((tm, LANES), lambda i: (i % spb, 0)),
            pl.BlockSpec((tm, LANES), lambda i: (i % spb, 0)),
            _const_spec((LANES, LANES)),
        ],
        out_specs=[
            pl.BlockSpec((tm, N_HEADS * HEAD_DIM), lambda i: (i, 0)),
            pl.BlockSpec((1, N_KV_HEADS, HEAD_DIM, tm), lambda i: (i // spb, 0, 0, i % spb)),
            pl.BlockSpec((tm, nkv), lambda i: (i, 0)),
        ],
        out_shape=[
            jax.ShapeDtypeStruct((t, N_HEADS * HEAD_DIM), BF16),
            jax.ShapeDtypeStruct((batch, N_KV_HEADS, HEAD_DIM, seq), BF16),
            jax.ShapeDtypeStruct((t, nkv), BF16),
        ],
        compiler_params=pltpu.CompilerParams(
            dimension_semantics=("arbitrary",), vmem_limit_bytes=VMEM_LIMIT_BYTES),
        name="qkv_proj",
    )(x2, g, w, qg, kg, cos, sin, gsum)


def _attn_kernel(q_ref, kt_ref, v_ref, o_ref, s_scr, p_scr):
    tq = q_ref.shape[0]
    seq = v_ref.shape[0]
    rows = GROUP * tq
    n_chunks = seq // ATTN_TK
    for j in range(2):
        q4 = jnp.concatenate(
            [q_ref[:, (GROUP * j + h) * HEAD_DIM:(GROUP * j + h + 1) * HEAD_DIM]
             for h in range(GROUP)], axis=0)

        def scores(c, mx):
            off = pl.multiple_of(c * ATTN_TK, ATTN_TK)
            s = jnp.dot(q4, kt_ref[0, j, :, pl.ds(off, ATTN_TK)], preferred_element_type=F32)
            s_scr[:, pl.ds(off, ATTN_TK)] = s
            for jj in range(ATTN_TK // LANES):
                mx = jnp.maximum(mx, s[:, jj * LANES:(jj + 1) * LANES])
            return mx

        mx = lax.fori_loop(0, n_chunks, scores, jnp.full((rows, LANES), -jnp.inf, F32))
        m = jnp.max(mx, axis=-1, keepdims=True)

        def probs(c, lsum):
            off = pl.multiple_of(c * ATTN_TK, ATTN_TK)
            p = jnp.exp2(s_scr[:, pl.ds(off, ATTN_TK)] - m)
            p_scr[:, pl.ds(off, ATTN_TK)] = p.astype(BF16)
            for jj in range(ATTN_TK // LANES):
                lsum = lsum + p[:, jj * LANES:(jj + 1) * LANES]
            return lsum

        lsum = lax.fori_loop(0, n_chunks, probs, jnp.zeros((rows, LANES), F32))
        inv_l = 1.0 / jnp.sum(lsum, axis=-1, keepdims=True)
        acc = jnp.dot(p_scr[...], v_ref[...], preferred_element_type=F32)
        out = acc[:, j * HEAD_DIM:(j + 1) * HEAD_DIM] * inv_l
        for h in range(GROUP):
            col = (GROUP * j + h) * HEAD_DIM
            o_ref[:, col:col + HEAD_DIM] = out[h * tq:(h + 1) * tq].astype(BF16)


def _attention(q, kt, v, batch, seq):
    t = q.shape[0]
    tq = ATTN_TQ
    nq = seq // tq
    pair_w = 2 * GROUP * HEAD_DIM
    return pl.pallas_call(
        _attn_kernel,
        grid=(batch, N_KV_HEADS // 2, nq),
        in_specs=[
            pl.BlockSpec((tq, pair_w), lambda b, p, i: (b * nq + i, p)),
            pl.BlockSpec((1, 2, HEAD_DIM, seq), lambda b, p, i: (b, p, 0, 0)),
            pl.BlockSpec((seq, 2 * HEAD_DIM), lambda b, p, i: (b, p)),
        ],
        out_specs=pl.BlockSpec((tq, pair_w), lambda b, p, i: (b * nq + i, p)),
        out_shape=jax.ShapeDtypeStruct((t, N_HEADS * HEAD_DIM), BF16),
        scratch_shapes=[
            pltpu.VMEM((GROUP * tq, seq), F32),
            pltpu.VMEM((GROUP * tq, seq), BF16),
        ],
        compiler_params=pltpu.CompilerParams(
            dimension_semantics=("arbitrary", "arbitrary", "arbitrary"),
            vmem_limit_bytes=VMEM_LIMIT_BYTES),
        name="attention",
    )(q, kt, v)


def _oproj_mlp_kernel(h_ref, o_ref, wo_ref, g_ref, w1_ref, w2_ref, gf_ref, out_ref, *, final):
    h1 = h_ref[...] + jnp.dot(o_ref[...], wo_ref[...], preferred_element_type=F32)
    hn = _rms(h1, g_ref[...]).astype(BF16)
    acc = h1
    for c in range(D_FF // FF_TILE):
        sl = slice(c * FF_TILE, (c + 1) * FF_TILE)
        u = jnp.dot(hn, w1_ref[:, sl], preferred_element_type=F32)
        a = jnp.square(jnp.maximum(u, 0.0)).astype(BF16)
        acc = acc + jnp.dot(a, w2_ref[sl, :], preferred_element_type=F32)
    if final:
        acc = _rms(acc, gf_ref[...])
    out_ref[...] = acc


def _oproj_mlp(h, o, wo, g, w1, w2, gf, final):
    t = h.shape[0]
    tm = ROW_TILE
    return pl.pallas_call(
        functools.partial(_oproj_mlp_kernel, final=final),
        grid=(t // tm,),
        in_specs=[
            pl.BlockSpec((tm, D_MODEL), lambda i: (i, 0)),
            pl.BlockSpec((tm, o.shape[1]), lambda i: (i, 0)),
            _const_spec(wo.shape),
            _const_spec((1, D_MODEL)),
            _const_spec(w1.shape),
            _const_spec(w2.shape),
            _const_spec((1, D_MODEL)),
        ],
        out_specs=pl.BlockSpec((tm, D_MODEL), lambda i: (i, 0)),
        out_shape=jax.ShapeDtypeStruct((t, D_MODEL), F32),
        compiler_params=pltpu.CompilerParams(
            dimension_semantics=("arbitrary",), vmem_limit_bytes=VMEM_LIMIT_BYTES),
        name="oproj_mlp_final" if final else "oproj_mlp",
    )(h, o, wo, g, w1, w2, gf)


def _gla_in_kernel(x_ref, g_ref, w_ref, wz_ref, q_ref, k_ref, v_ref, r_ref, z_ref):
    xn = _rms(x_ref[...], g_ref[...]).astype(BF16)
    qk = jnp.dot(xn, w_ref[:, :2 * GLA_QK], preferred_element_type=F32)
    q_ref[...] = (qk[:, :GLA_QK] * (GLA_DK ** -0.5)).astype(BF16)
    k_ref[...] = qk[:, GLA_QK:].astype(BF16)
    v_ref[...] = jnp.dot(xn, w_ref[:, 2 * GLA_QK:2 * GLA_QK + GLA_V],
                         preferred_element_type=F32).astype(BF16)
    r_ref[...] = jnp.dot(xn, w_ref[:, 2 * GLA_QK + GLA_V:],
                         preferred_element_type=F32).astype(BF16)
    z_ref[...] = jnp.dot(xn, wz_ref[...], preferred_element_type=F32)


def _gla_inproj(h, g, w_main, w_z):
    t = h.shape[0]
    tm = ROW_TILE
    row = lambda i: (i, 0)
    return pl.pallas_call(
        _gla_in_kernel,
        grid=(t // tm,),
        in_specs=[
            pl.BlockSpec((tm, D_MODEL), row),
            _const_spec((1, D_MODEL)),
            _const_spec(w_main.shape),
            _const_spec(w_z.shape),
        ],
        out_specs=[
            pl.BlockSpec((tm, GLA_QK), row),
            pl.BlockSpec((tm, GLA_QK), row),
            pl.BlockSpec((tm, GLA_V), row),
            pl.BlockSpec((tm, GLA_V), row),
            pl.BlockSpec((tm, LANES), row),
        ],
        out_shape=[
            jax.ShapeDtypeStruct((t, GLA_QK), BF16),
            jax.ShapeDtypeStruct((t, GLA_QK), BF16),
            jax.ShapeDtypeStruct((t, GLA_V), BF16),
            jax.ShapeDtypeStruct((t, GLA_V), BF16),
            jax.ShapeDtypeStruct((t, LANES), F32),
        ],
        compiler_params=pltpu.CompilerParams(
            dimension_semantics=("arbitrary",), vmem_limit_bytes=VMEM_LIMIT_BYTES),
        name="gla_inproj",
    )(h, g, w_main, w_z)


def _gla_kernel(q_ref, k_ref, v_ref, r_ref, z_ref, wup_ref, bias_ref, og_ref,
                tri_ref, ones_ref, expand_ref, out_ref,
                qe_scr, ke_scr, kd_scr, dec_scr, o_scr, st_scr):
    seq = q_ref.shape[0]
    c64 = GLA_CHUNK
    n_chunks = seq // c64
    pre = GLA_PRE
    per = pre // c64

    def precompute(i, carry):
        off = pl.multiple_of(i * pre, pre)
        rows = pl.ds(off, pre)
        zb = z_ref[rows, :]
        zb_hi = zb.astype(BF16)
        zb_lo = (zb - zb_hi.astype(F32)).astype(BF16)
        qf = q_ref[rows, :].astype(F32)
        kf = k_ref[rows, :].astype(F32)
        for d in range(2):
            wu = wup_ref[d]
            wu_hi = wu.astype(BF16)
            wu_lo = (wu - wu_hi.astype(F32)).astype(BF16)
            logit = (jnp.dot(zb_hi, wu_hi, preferred_element_type=F32)
                     + jnp.dot(zb_lo, wu_hi, preferred_element_type=F32)
                     + jnp.dot(zb_hi, wu_lo, preferred_element_type=F32)
                     + bias_ref[d])
            lg = (jnp.minimum(logit, 0.0) - jnp.log1p(jnp.exp(-jnp.abs(logit)))) * (1.0 / GATE_TAU)
            b = _split_dot(tri_ref[d], lg)
            tot = _split_dot(ones_ref[...], lg)
            qe_scr[d, rows, :] = (qf * jnp.exp(b)).astype(BF16)
            ke_scr[d, rows, :] = (kf * jnp.exp(-b)).astype(BF16)
            kd_scr[d, rows, :] = (kf * jnp.exp(tot - b)).astype(BF16)
            lgt = lg.T
            lgt_hi = lgt.astype(BF16)
            lgt_lo = (lgt - lgt_hi.astype(F32)).astype(BF16)
            tot_t = (jnp.dot(lgt_hi, expand_ref[...], preferred_element_type=F32)
                     + jnp.dot(lgt_lo, expand_ref[...], preferred_element_type=F32))
            dec = jnp.exp(tot_t)
            for cc in range(per):
                dec_scr[d, i * per + cc] = dec[:, cc * LANES:(cc + 1) * LANES]
        return carry

    lax.fori_loop(0, seq // pre, precompute, 0)

    o_scr[...] = jnp.zeros_like(o_scr)
    st_scr[...] = jnp.zeros_like(st_scr)
    ridx = lax.broadcasted_iota(jnp.int32, (c64, c64), 0)
    cidx = lax.broadcasted_iota(jnp.int32, (c64, c64), 1)
    masks = (ridx >= cidx, ridx <= cidx)

    def scan(i, carry):
        for d in range(2):
            c = i if d == 0 else n_chunks - 1 - i
            off = pl.multiple_of(c * c64, c64)
            rows = pl.ds(off, c64)
            qe = qe_scr[d, rows, :]
            ke = ke_scr[d, rows, :]
            kd = kd_scr[d, rows, :]
            vv = v_ref[rows, :]
            a = lax.dot_general(qe, ke, (((1,), (1,)), ((), ())), preferred_element_type=F32)
            a = jnp.where(masks[d], a, 0.0).astype(BF16)
            st = st_scr[d]
            o = (jnp.dot(a, vv, preferred_element_type=F32)
                 + jnp.dot(qe, st.astype(BF16), preferred_element_type=F32))
            o_scr[rows, :] = o_scr[rows, :] + o
            kv = lax.dot_general(kd, vv, (((0,), (0,)), ((), ())), preferred_element_type=F32)
            dec = dec_scr[d, c]
            st_scr[d] = jnp.concatenate([dec, dec], axis=1) * st + kv
        return carry

    lax.fori_loop(0, n_chunks, scan, 0)

    def finish(i, carry):
        off = pl.multiple_of(i * pre, pre)
        rows = pl.ds(off, pre)
        on = _rms(o_scr[rows, :], og_ref[...])
        rr = r_ref[rows, :].astype(F32)
        out_ref[rows, :] = (on * (rr * jax.nn.sigmoid(rr))).astype(BF16)
        return carry

    lax.fori_loop(0, seq // pre, finish, 0)


def _gla_scan(q, k, v, r, z, wup, bias, og, tri, ones_bd, expand, batch, seq):
    t = q.shape[0]
    n_chunks = seq // GLA_CHUNK
    bh = lambda b, h: (b, h)
    return pl.pallas_call(
        _gla_kernel,
        grid=(batch, GLA_HEADS),
        in_specs=[
            pl.BlockSpec((seq, GLA_DK), bh),
            pl.BlockSpec((seq, GLA_DK), bh),
            pl.BlockSpec((seq, GLA_DV), bh),
            pl.BlockSpec((seq, GLA_DV), bh),
            pl.BlockSpec((seq, LANES), lambda b, h: (b, 0)),
            pl.BlockSpec((2, LANES, GLA_DK), lambda b, h: (0, 0, h)),
            pl.BlockSpec((2, 1, GLA_DK), lambda b, h: (0, 0, h)),
            _const_spec((1, GLA_DV)),
            _const_spec(tri.shape),
            _const_spec(ones_bd.shape),
            _const_spec(expand.shape),
        ],
        out_specs=pl.BlockSpec((seq, GLA_DV), bh),
        out_shape=jax.ShapeDtypeStruct((t, GLA_V), BF16),
        scratch_shapes=[
            pltpu.VMEM((2, seq, GLA_DK), BF16),
            pltpu.VMEM((2, seq, GLA_DK), BF16),
            pltpu.VMEM((2, seq, GLA_DK), BF16),
            pltpu.VMEM((2, n_chunks, GLA_DK, LANES), F32),
            pltpu.VMEM((seq, GLA_DV), F32),
            pltpu.VMEM((2, GLA_DK, GLA_DV), F32),
        ],
        compiler_params=pltpu.CompilerParams(
            dimension_semantics=("arbitrary", "arbitrary"),
            vmem_limit_bytes=VMEM_LIMIT_BYTES),
        name="gla_scan",
    )(q, k, v, r, z, wup, bias, og, tri, ones_bd, expand)


def _rope_tables(seq):
    pos = jnp.arange(seq, dtype=jnp.int32)
    half = HEAD_DIM // 2
    inv = ROPE_THETA ** (-jnp.arange(0, half, 2, dtype=F32) / half)
    ang_r = (pos // GRID_W).astype(F32)[:, None] * inv[None, :]
    ang_c = (pos % GRID_W).astype(F32)[:, None] * inv[None, :]
    cos_h = jnp.concatenate([jnp.cos(ang_r)] * 2 + [jnp.cos(ang_c)] * 2, axis=-1)
    sin_h = jnp.concatenate([-jnp.sin(ang_r), jnp.sin(ang_r), -jnp.sin(ang_c), jnp.sin(ang_c)], axis=-1)
    return jnp.tile(cos_h, (1, 2)), jnp.tile(sin_h, (1, 2))


def _gla_constants():
    n = GLA_PRE
    r = jnp.arange(n)[:, None]
    c = jnp.arange(n)[None, :]
    same = (r // GLA_CHUNK) == (c // GLA_CHUNK)
    tri = jnp.stack([same & (c <= r), same & (c >= r)]).astype(BF16)
    ones_bd = same.astype(BF16)
    per = n // GLA_CHUNK
    col_chunk = jnp.arange(per * LANES)[None, :] // LANES
    expand = ((r // GLA_CHUNK) == col_chunk).astype(BF16)
    return tri, ones_bd, expand


def kernel(x, norm_mix, norm_mlp, attn_w_qkv, attn_q_norm, attn_k_norm, attn_w_o,
           gla_w_in, gla_w_gate_up, gla_b_gate, gla_out_norm, gla_w_o,
           mlp_w_in, mlp_w_out, final_norm):
    batch, seq, d = x.shape
    t = batch * seq
    h = x.reshape(t, d)
    row = lambda a: a.reshape(1, -1).astype(F32)

    cos, sin = _rope_tables(seq)
    gidx = jnp.arange(LANES)
    gsum = (gidx[:, None] // HEAD_DIM == gidx[None, :] // HEAD_DIM).astype(BF16)
    q_gain = jnp.tile(attn_q_norm[0], 2).reshape(1, LANES) * (LOG2E / math.sqrt(HEAD_DIM))
    k_gain = jnp.tile(attn_k_norm[0], 2).reshape(1, LANES)
    q, kt, v = _qkv_proj(h, row(norm_mix[0]), attn_w_qkv[0].astype(BF16), q_gain, k_gain,
                         cos, sin, gsum, batch, seq)
    o = _attention(q, kt, v, batch, seq)
    h = _oproj_mlp(h, o, attn_w_o[0].astype(BF16), row(norm_mlp[0]),
                   mlp_w_in[0].astype(BF16), mlp_w_out[0].astype(BF16), row(final_norm), False)

    n_main = 2 * GLA_QK + 2 * GLA_V
    w_in = gla_w_in[0]
    w_main = w_in[:, :n_main].astype(BF16)
    w_z = jnp.pad(w_in[:, n_main:], ((0, 0), (0, LANES - 2 * GATE_RANK))).astype(BF16)
    gq, gk, gv, gr, gz = _gla_inproj(h, row(norm_mix[1]), w_main, w_z)
    wup = jnp.zeros((2, LANES, GLA_QK), F32)
    wup = wup.at[0, :GATE_RANK].set(gla_w_gate_up[0, 0])
    wup = wup.at[1, GATE_RANK:2 * GATE_RANK].set(gla_w_gate_up[0, 1])
    tri, ones_bd, expand = _gla_constants()
    go = _gla_scan(gq, gk, gv, gr, gz, wup, gla_b_gate[0].reshape(2, 1, GLA_QK),
                   row(gla_out_norm[0]), tri, ones_bd, expand, batch, seq)
    h = _oproj_mlp(h, go, gla_w_o[0].astype(BF16), row(norm_mlp[1]),
                   mlp_w_in[1].astype(BF16), mlp_w_out[1].astype(BF16), row(final_norm), True)
    return h.reshape(batch, seq, d)
```

```python
import functools
import math

import jax
import jax.numpy as jnp
from jax import lax
from jax.experimental import pallas as pl
from jax.experimental.pallas import tpu as pltpu

F32 = jnp.float32
BF16 = jnp.bfloat16

D_MODEL = 1024
GRID_W = 64
N_HEADS = 16
N_KV_HEADS = 4
HEAD_DIM = 64
GROUP = N_HEADS // N_KV_HEADS
ROPE_THETA = 10000.0
GLA_HEADS = 4
GLA_DK = 128
GLA_DV = 256
GLA_QK = GLA_HEADS * GLA_DK
GLA_V = GLA_HEADS * GLA_DV
GATE_RANK = 16
GATE_TAU = 16.0
D_FF = 4 * D_MODEL
EPS = 1e-6

LANES = 128
VMEM_LIMIT_BYTES = 56 * 1024 * 1024

ROW_TILE = 512
ATTN_TQ = 256
ATTN_TK = 512
FF_TILE = 1024
GLA_CHUNK = 64
GLA_PRE = 256
LOG2E = 1.4426950408889634


def _const_spec(shape):
    nd = len(shape)
    return pl.BlockSpec(shape, lambda *_: (0,) * nd, pipeline_mode=pl.Buffered(1))


def _rms(x, gain):
    ms = jnp.mean(x * x, axis=-1, keepdims=True)
    return x * lax.rsqrt(ms + EPS) * gain


def _split_dot(a_bf16, x_f32):
    hi = x_f32.astype(BF16)
    lo = (x_f32 - hi.astype(F32)).astype(BF16)
    return (jnp.dot(a_bf16, hi, preferred_element_type=F32)
            + jnp.dot(a_bf16, lo, preferred_element_type=F32))


def _qkv_kernel(x_ref, g_ref, w_ref, qg_ref, kg_ref, cos_ref, sin_ref, gsum_ref,
                q_ref, kt_ref, v_ref):
    tm = x_ref.shape[0]
    xn = _rms(x_ref[...], g_ref[...]).astype(BF16)
    y = jnp.dot(xn, w_ref[...], preferred_element_type=F32)
    cos = cos_ref[...]
    sin = sin_ref[...]
    lane = lax.broadcasted_iota(jnp.int32, (tm, LANES), 1)
    first_half = (lane % 32) < 16
    gsum = gsum_ref[...]

    def norm_rope(c, gain):
        ss = jnp.dot((c * c).astype(BF16), gsum, preferred_element_type=F32)
        cn = c * lax.rsqrt(ss * (1.0 / HEAD_DIM) + EPS) * gain
        rot = jnp.where(first_half, pltpu.roll(cn, LANES - 16, 1), pltpu.roll(cn, 16, 1))
        return cn * cos + rot * sin

    nq = N_HEADS * HEAD_DIM
    for j in range(nq // LANES):
        sl = slice(j * LANES, (j + 1) * LANES)
        q_ref[:, sl] = norm_rope(y[:, sl], qg_ref[...]).astype(BF16)
    for j in range(N_KV_HEADS * HEAD_DIM // LANES):
        kc = norm_rope(y[:, nq + j * LANES: nq + (j + 1) * LANES], kg_ref[...])
        kt = kc.T
        kt_ref[0, 2 * j] = kt[:HEAD_DIM].astype(BF16)
        kt_ref[0, 2 * j + 1] = kt[HEAD_DIM:].astype(BF16)
    ones_half = ((lane >= HEAD_DIM).astype(F32))[:1]
    for g in range(N_KV_HEADS):
        c0 = nq + N_KV_HEADS * HEAD_DIM + g * LANES
        v_ref[:, g * LANES:(g + 1) * LANES] = (y[:, c0:c0 + LANES] + ones_half).astype(BF16)


def _qkv_proj(x2, g, w, qg, kg, cos, sin, gsum, batch, seq):
    t = x2.shape[0]
    tm = ROW_TILE
    spb = seq // tm
    nkv = N_KV_HEADS * LANES
    return pl.pallas_call(
        _qkv_kernel,
        grid=(t // tm,),
        in_specs=[
            pl.BlockSpec((tm, D_MODEL), lambda i: (i, 0)),
            _const_spec((1, D_MODEL)),
            _const_spec(w.shape),
            _const_spec((1, LANES)),
            _const_spec((1, LANES)),
            pl.BlockSpec((tm, LANES), lambda i: (i % spb, 0)),
            pl.BlockSpec((tm, LANES), lambda i: (i % spb, 0)),
            _const_spec((LANES, LANES)),
        ],
        out_specs=[
            pl.BlockSpec((tm, N_HEADS * HEAD_DIM), lambda i: (i, 0)),
            pl.BlockSpec((1, N_KV_HEADS, HEAD_DIM, tm), lambda i: (i // spb, 0, 0, i % spb)),
            pl.BlockSpec((tm, nkv), lambda i: (i, 0)),
        ],
        out_shape=[
            jax.ShapeDtypeStruct((t, N_HEADS * HEAD_DIM), BF16),
            jax.ShapeDtypeStruct((batch, N_KV_HEADS, HEAD_DIM, seq), BF16),
            jax.ShapeDtypeStruct((t, nkv), BF16),
        ],
        compiler_params=pltpu.CompilerParams(
            dimension_semantics=("arbitrary",), vmem_limit_bytes=VMEM_LIMIT_BYTES),
        name="qkv_proj",
    )(x2, g, w, qg, kg, cos, sin, gsum)


def _attn_kernel(q_ref, kt_ref, v_ref, o_ref):
    tq = q_ref.shape[0]
    seq = v_ref.shape[0]
    q4 = jnp.concatenate(
        [q_ref[:, h * HEAD_DIM:(h + 1) * HEAD_DIM] for h in range(GROUP)], axis=0)
    m = None
    acc = None
    for c in range(seq // ATTN_TK):
        ks = slice(c * ATTN_TK, (c + 1) * ATTN_TK)
        s = jnp.dot(q4, kt_ref[0, 0, :, ks], preferred_element_type=F32)
        m_c = jnp.max(s, axis=-1, keepdims=True)
        m_new = m_c if m is None else jnp.maximum(m, m_c)
        p = jnp.exp2(s - m_new).astype(BF16)
        pv = jnp.dot(p, v_ref[ks, :], preferred_element_type=F32)
        acc = pv if acc is None else acc * jnp.exp2(m - m_new) + pv
        m = m_new
    out = acc * (1.0 / pltpu.roll(acc, HEAD_DIM, 1))
    lane = lax.broadcasted_iota(jnp.int32, (tq, LANES), 1)
    for hp in range(GROUP // 2):
        even = out[(2 * hp) * tq:(2 * hp + 1) * tq]
        odd = pltpu.roll(out[(2 * hp + 1) * tq:(2 * hp + 2) * tq], HEAD_DIM, 1)
        o_ref[:, hp * LANES:(hp + 1) * LANES] = jnp.where(lane < HEAD_DIM, even, odd).astype(BF16)


def _attention(q, kt, v, batch, seq):
    t = q.shape[0]
    tq = ATTN_TQ
    nq = seq // tq
    grp_w = GROUP * HEAD_DIM
    return pl.pallas_call(
        _attn_kernel,
        grid=(batch, N_KV_HEADS, nq),
        in_specs=[
            pl.BlockSpec((tq, grp_w), lambda b, g, i: (b * nq + i, g)),
            pl.BlockSpec((1, 1, HEAD_DIM, seq), lambda b, g, i: (b, g, 0, 0)),
            pl.BlockSpec((seq, LANES), lambda b, g, i: (b, g)),
        ],
        out_specs=pl.BlockSpec((tq, grp_w), lambda b, g, i: (b * nq + i, g)),
        out_shape=jax.ShapeDtypeStruct((t, N_HEADS * HEAD_DIM), BF16),
        compiler_params=pltpu.CompilerParams(
            dimension_semantics=("arbitrary", "arbitrary", "arbitrary"),
            vmem_limit_bytes=VMEM_LIMIT_BYTES),
        name="attention",
    )(q, kt, v)


def _oproj_mlp_kernel(h_ref, o_ref, wo_ref, g_ref, w1_ref, w2_ref, gf_ref, out_ref, *, final):
    h1 = h_ref[...] + jnp.dot(o_ref[...], wo_ref[...], preferred_element_type=F32)
    hn = _rms(h1, g_ref[...]).astype(BF16)
    acc = h1
    for c in range(D_FF // FF_TILE):
        sl = slice(c * FF_TILE, (c + 1) * FF_TILE)
        u = jnp.dot(hn, w1_ref[:, sl], preferred_element_type=F32)
        a = jnp.square(jnp.maximum(u, 0.0)).astype(BF16)
        acc = acc + jnp.dot(a, w2_ref[sl, :], preferred_element_type=F32)
    if final:
        acc = _rms(acc, gf_ref[...])
    out_ref[...] = acc


def _oproj_mlp(h, o, wo, g, w1, w2, gf, final):
    t = h.shape[0]
    tm = ROW_TILE
    return pl.pallas_call(
        functools.partial(_oproj_mlp_kernel, final=final),
        grid=(t // tm,),
        in_specs=[
            pl.BlockSpec((tm, D_MODEL), lambda i: (i, 0)),
            pl.BlockSpec((tm, o.shape[1]), lambda i: (i, 0)),
            _const_spec(wo.shape),
            _const_spec((1, D_MODEL)),
            _const_spec(w1.shape),
            _const_spec(w2.shape),
            _const_spec((1, D_MODEL)),
        ],
        out_specs=pl.BlockSpec((tm, D_MODEL), lambda i: (i, 0)),
        out_shape=jax.ShapeDtypeStruct((t, D_MODEL), F32),
        compiler_params=pltpu.CompilerParams(
            dimension_semantics=("arbitrary",), vmem_limit_bytes=VMEM_LIMIT_BYTES),
        name="oproj_mlp_final" if final else "oproj_mlp",
    )(h, o, wo, g, w1, w2, gf)


def _gla_in_kernel(x_ref, g_ref, w_ref, wz_ref, q_ref, k_ref, v_ref, r_ref, z_ref):
    xn = _rms(x_ref[...], g_ref[...]).astype(BF16)
    qk = jnp.dot(xn, w_ref[:, :2 * GLA_QK], preferred_element_type=F32)
    q_ref[...] = (qk[:, :GLA_QK] * (GLA_DK ** -0.5)).astype(BF16)
    k_ref[...] = qk[:, GLA_QK:].astype(BF16)
    v_ref[...] = jnp.dot(xn, w_ref[:, 2 * GLA_QK:2 * GLA_QK + GLA_V],
                         preferred_element_type=F32).astype(BF16)
    r_ref[...] = jnp.dot(xn, w_ref[:, 2 * GLA_QK + GLA_V:],
                         preferred_element_type=F32).astype(BF16)
    z_ref[...] = jnp.dot(xn, wz_ref[...], preferred_element_type=F32)


def _gla_inproj(h, g, w_main, w_z):
    t = h.shape[0]
    tm = ROW_TILE
    row = lambda i: (i, 0)
    return pl.pallas_call(
        _gla_in_kernel,
        grid=(t // tm,),
        in_specs=[
            pl.BlockSpec((tm, D_MODEL), row),
            _const_spec((1, D_MODEL)),
            _const_spec(w_main.shape),
            _const_spec(w_z.shape),
        ],
        out_specs=[
            pl.BlockSpec((tm, GLA_QK), row),
            pl.BlockSpec((tm, GLA_QK), row),
            pl.BlockSpec((tm, GLA_V), row),
            pl.BlockSpec((tm, GLA_V), row),
            pl.BlockSpec((tm, LANES), row),
        ],
        out_shape=[
            jax.ShapeDtypeStruct((t, GLA_QK), BF16),
            jax.ShapeDtypeStruct((t, GLA_QK), BF16),
            jax.ShapeDtypeStruct((t, GLA_V), BF16),
            jax.ShapeDtypeStruct((t, GLA_V), BF16),
            jax.ShapeDtypeStruct((t, LANES), F32),
        ],
        compiler_params=pltpu.CompilerParams(
            dimension_semantics=("arbitrary",), vmem_limit_bytes=VMEM_LIMIT_BYTES),
        name="gla_inproj",
    )(h, g, w_main, w_z)


def _gla_kernel(q_ref, k_ref, v_ref, r_ref, z_ref, wup_ref, bias_ref, og_ref,
                tri_ref, ones_ref, expand_ref, out_ref,
                qe_scr, ke_scr, kd_scr, dec_scr, o_scr, st_scr):
    seq = q_ref.shape[0]
    c64 = GLA_CHUNK
    n_chunks = seq // c64
    pre = GLA_PRE
    per = pre // c64

    def precompute(i, carry):
        off = pl.multiple_of(i * pre, pre)
        rows = pl.ds(off, pre)
        zb = z_ref[rows, :]
        zb_hi = zb.astype(BF16)
        zb_lo = (zb - zb_hi.astype(F32)).astype(BF16)
        qf = q_ref[rows, :].astype(F32)
        kf = k_ref[rows, :].astype(F32)
        for d in range(2):
            wu = wup_ref[d]
            wu_hi = wu.astype(BF16)
            wu_lo = (wu - wu_hi.astype(F32)).astype(BF16)
            logit = (jnp.dot(zb_hi, wu_hi, preferred_element_type=F32)
                     + jnp.dot(zb_lo, wu_hi, preferred_element_type=F32)
                     + jnp.dot(zb_hi, wu_lo, preferred_element_type=F32)
                     + bias_ref[d])
            lg = (jnp.minimum(logit, 0.0) - jnp.log1p(jnp.exp(-jnp.abs(logit)))) * (1.0 / GATE_TAU)
            b = _split_dot(tri_ref[d], lg)
            tot = _split_dot(ones_ref[...], lg)
            qe_scr[d, rows, :] = (qf * jnp.exp(b)).astype(BF16)
            ke_scr[d, rows, :] = (kf * jnp.exp(-b)).astype(BF16)
            kd_scr[d, rows, :] = (kf * jnp.exp(tot - b)).astype(BF16)
            lgt = lg.T
            lgt_hi = lgt.astype(BF16)
            lgt_lo = (lgt - lgt_hi.astype(F32)).astype(BF16)
            tot_t = (jnp.dot(lgt_hi, expand_ref[...], preferred_element_type=F32)
                     + jnp.dot(lgt_lo, expand_ref[...], preferred_element_type=F32))
            dec = jnp.exp(tot_t)
            for cc in range(per):
                dec_scr[d, i * per + cc] = dec[:, cc * LANES:(cc + 1) * LANES]
        return carry

    lax.fori_loop(0, seq // pre, precompute, 0)

    o_scr[...] = jnp.zeros_like(o_scr)
    st_scr[...] = jnp.zeros_like(st_scr)
    ridx = lax.broadcasted_iota(jnp.int32, (c64, c64), 0)
    cidx = lax.broadcasted_iota(jnp.int32, (c64, c64), 1)
    masks = (ridx >= cidx, ridx <= cidx)

    def scan(i, carry):
        for d in range(2):
            c = i if d == 0 else n_chunks - 1 - i
            off = pl.multiple_of(c * c64, c64)
            rows = pl.ds(off, c64)
            qe = qe_scr[d, rows, :]
            ke = ke_scr[d, rows, :]
            kd = kd_scr[d, rows, :]
            vv = v_ref[rows, :]
            a = lax.dot_general(qe, ke, (((1,), (1,)), ((), ())), preferred_element_type=F32)
            a = jnp.where(masks[d], a, 0.0).astype(BF16)
            st = st_scr[d]
            o = (jnp.dot(a, vv, preferred_element_type=F32)
                 + jnp.dot(qe, st.astype(BF16), preferred_element_type=F32))
            o_scr[rows, :] = o_scr[rows, :] + o
            kv = lax.dot_general(kd, vv, (((0,), (0,)), ((), ())), preferred_element_type=F32)
            dec = dec_scr[d, c]
            st_scr[d] = jnp.concatenate([dec, dec], axis=1) * st + kv
        return carry

    lax.fori_loop(0, n_chunks, scan, 0)

    def finish(i, carry):
        off = pl.multiple_of(i * pre, pre)
        rows = pl.ds(off, pre)
        on = _rms(o_scr[rows, :], og_ref[...])
        rr = r_ref[rows, :].astype(F32)
        out_ref[rows, :] = (on * (rr * jax.nn.sigmoid(rr))).astype(BF16)
        return carry

    lax.fori_loop(0, seq // pre, finish, 0)


def _gla_scan(q, k, v, r, z, wup, bias, og, tri, ones_bd, expand, batch, seq):
    t = q.shape[0]
    n_chunks = seq // GLA_CHUNK
    bh = lambda b, h: (b, h)
    return pl.pallas_call(
        _gla_kernel,
        grid=(batch, GLA_HEADS),
        in_specs=[
            pl.BlockSpec((seq, GLA_DK), bh),
            pl.BlockSpec((seq, GLA_DK), bh),
            pl.BlockSpec((seq, GLA_DV), bh),
            pl.BlockSpec((seq, GLA_DV), bh),
            pl.BlockSpec((seq, LANES), lambda b, h: (b, 0)),
            pl.BlockSpec((2, LANES, GLA_DK), lambda b, h: (0, 0, h)),
            pl.BlockSpec((2, 1, GLA_DK), lambda b, h: (0, 0, h)),
            _const_spec((1, GLA_DV)),
            _const_spec(tri.shape),
            _const_spec(ones_bd.shape),
            _const_spec(expand.shape),
        ],
        out_specs=pl.BlockSpec((seq, GLA_DV), bh),
        out_shape=jax.ShapeDtypeStruct((t, GLA_V), BF16),
        scratch_shapes=[
            pltpu.VMEM((2, seq, GLA_DK), BF16),
            pltpu.VMEM((2, seq, GLA_DK), BF16),
            pltpu.VMEM((2, seq, GLA_DK), BF16),
            pltpu.VMEM((2, n_chunks, GLA_DK, LANES), F32),
            pltpu.VMEM((seq, GLA_DV), F32),
            pltpu.VMEM((2, GLA_DK, GLA_DV), F32),
        ],
        compiler_params=pltpu.CompilerParams(
            dimension_semantics=("arbitrary", "arbitrary"),
            vmem_limit_bytes=VMEM_LIMIT_BYTES),
        name="gla_scan",
    )(q, k, v, r, z, wup, bias, og, tri, ones_bd, expand)


def _rope_tables(seq):
    pos = jnp.arange(seq, dtype=jnp.int32)
    half = HEAD_DIM // 2
    inv = ROPE_THETA ** (-jnp.arange(0, half, 2, dtype=F32) / half)
    ang_r = (pos // GRID_W).astype(F32)[:, None] * inv[None, :]
    ang_c = (pos % GRID_W).astype(F32)[:, None] * inv[None, :]
    cos_h = jnp.concatenate([jnp.cos(ang_r)] * 2 + [jnp.cos(ang_c)] * 2, axis=-1)
    sin_h = jnp.concatenate([-jnp.sin(ang_r), jnp.sin(ang_r), -jnp.sin(ang_c), jnp.sin(ang_c)], axis=-1)
    return jnp.tile(cos_h, (1, 2)), jnp.tile(sin_h, (1, 2))


def _gla_constants():
    n = GLA_PRE
    r = jnp.arange(n)[:, None]
    c = jnp.arange(n)[None, :]
    same = (r // GLA_CHUNK) == (c // GLA_CHUNK)
    tri = jnp.stack([same & (c <= r), same & (c >= r)]).astype(BF16)
    ones_bd = same.astype(BF16)
    per = n // GLA_CHUNK
    col_chunk = jnp.arange(per * LANES)[None, :] // LANES
    expand = ((r // GLA_CHUNK) == col_chunk).astype(BF16)
    return tri, ones_bd, expand


def kernel(x, norm_mix, norm_mlp, attn_w_qkv, attn_q_norm, attn_k_norm, attn_w_o,
           gla_w_in, gla_w_gate_up, gla_b_gate, gla_out_norm, gla_w_o,
           mlp_w_in, mlp_w_out, final_norm):
    batch, seq, d = x.shape
    t = batch * seq
    h = x.reshape(t, d)
    row = lambda a: a.reshape(1, -1).astype(F32)

    cos, sin = _rope_tables(seq)
    gidx = jnp.arange(LANES)
    gsum = (gidx[:, None] // HEAD_DIM == gidx[None, :] // HEAD_DIM).astype(BF16)
    q_gain = jnp.tile(attn_q_norm[0], 2).reshape(1, LANES) * (LOG2E / math.sqrt(HEAD_DIM))
    k_gain = jnp.tile(attn_k_norm[0], 2).reshape(1, LANES)
    n_qk = (N_HEADS + N_KV_HEADS) * HEAD_DIM
    w_v = attn_w_qkv[0][:, n_qk:].reshape(d, N_KV_HEADS, HEAD_DIM)
    w_v = jnp.pad(w_v, ((0, 0), (0, 0), (0, LANES - HEAD_DIM))).reshape(d, N_KV_HEADS * LANES)
    w_qkv = jnp.concatenate([attn_w_qkv[0][:, :n_qk], w_v], axis=1).astype(BF16)
    q, kt, v = _qkv_proj(h, row(norm_mix[0]), w_qkv, q_gain, k_gain,
                         cos, sin, gsum, batch, seq)
    o = _attention(q, kt, v, batch, seq)
    h = _oproj_mlp(h, o, attn_w_o[0].astype(BF16), row(norm_mlp[0]),
                   mlp_w_in[0].astype(BF16), mlp_w_out[0].astype(BF16), row(final_norm), False)

    n_main = 2 * GLA_QK + 2 * GLA_V
    w_in = gla_w_in[0]
    w_main = w_in[:, :n_main].astype(BF16)
    w_z = jnp.pad(w_in[:, n_main:], ((0, 0), (0, LANES - 2 * GATE_RANK))).astype(BF16)
    gq, gk, gv, gr, gz = _gla_inproj(h, row(norm_mix[1]), w_main, w_z)
    wup = jnp.zeros((2, LANES, GLA_QK), F32)
    wup = wup.at[0, :GATE_RANK].set(gla_w_gate_up[0, 0])
    wup = wup.at[1, GATE_RANK:2 * GATE_RANK].set(gla_w_gate_up[0, 1])
    tri, ones_bd, expand = _gla_constants()
    go = _gla_scan(gq, gk, gv, gr, gz, wup, gla_b_gate[0].reshape(2, 1, GLA_QK),
                   row(gla_out_norm[0]), tri, ones_bd, expand, batch, seq)
    h = _oproj_mlp(h, go, gla_w_o[0].astype(BF16), row(norm_mlp[1]),
                   mlp_w_in[1].astype(BF16), mlp_w_out[1].astype(BF16), row(final_norm), True)
    return h.reshape(batch, seq, d)
```

```python
import functools
import math

import jax
import jax.numpy as jnp
from jax import lax
from jax.experimental import pallas as pl
from jax.experimental.pallas import tpu as pltpu

F32 = jnp.float32
BF16 = jnp.bfloat16

D_MODEL = 1024
GRID_W = 64
N_HEADS = 16
N_KV_HEADS = 4
HEAD_DIM = 64
GROUP = N_HEADS // N_KV_HEADS
ROPE_THETA = 10000.0
GLA_HEADS = 4
GLA_DK = 128
GLA_DV = 256
GLA_QK = GLA_HEADS * GLA_DK
GLA_V = GLA_HEADS * GLA_DV
GATE_RANK = 16
GATE_TAU = 16.0
D_FF = 4 * D_MODEL
EPS = 1e-6

LANES = 128
VMEM_LIMIT_BYTES = 56 * 1024 * 1024

ROW_TILE = 512
ATTN_TQ = 256
ATTN_TK = 512
FF_TILE = 1024
GLA_CHUNK = 64
GLA_PRE = 256
GLA_PRE_UNROLL = 2
GLA_SCAN_UNROLL = 4
LOG2E = 1.4426950408889634


def _const_spec(shape):
    nd = len(shape)
    return pl.BlockSpec(shape, lambda *_: (0,) * nd, pipeline_mode=pl.Buffered(1))


def _rms(x, gain):
    ms = jnp.mean(x * x, axis=-1, keepdims=True)
    return x * lax.rsqrt(ms + EPS) * gain


def _split_dot(a_bf16, x_f32):
    hi = x_f32.astype(BF16)
    lo = (x_f32 - hi.astype(F32)).astype(BF16)
    return (jnp.dot(a_bf16, hi, preferred_element_type=F32)
            + jnp.dot(a_bf16, lo, preferred_element_type=F32))


def _qkv_kernel(x_ref, g_ref, w_ref, qg_ref, kg_ref, cos_ref, sin_ref, gsum_ref,
                q_ref, kt_ref, v_ref):
    tm = x_ref.shape[0]
    xn = _rms(x_ref[...], g_ref[...]).astype(BF16)
    y = jnp.dot(xn, w_ref[...], preferred_element_type=F32)
    cos = cos_ref[...]
    sin = sin_ref[...]
    lane = lax.broadcasted_iota(jnp.int32, (tm, LANES), 1)
    first_half = (lane % 32) < 16
    gsum = gsum_ref[...]

    def norm_rope(c, gain):
        ss = jnp.dot((c * c).astype(BF16), gsum, preferred_element_type=F32)
        cn = c * lax.rsqrt(ss * (1.0 / HEAD_DIM) + EPS) * gain
        rot = jnp.where(first_half, pltpu.roll(cn, LANES - 16, 1), pltpu.roll(cn, 16, 1))
        return cn * cos + rot * sin

    nq = N_HEADS * HEAD_DIM
    for j in range(nq // LANES):
        sl = slice(j * LANES, (j + 1) * LANES)
        q_ref[:, sl] = norm_rope(y[:, sl], qg_ref[...]).astype(BF16)
    for j in range(N_KV_HEADS * HEAD_DIM // LANES):
        kc = norm_rope(y[:, nq + j * LANES: nq + (j + 1) * LANES], kg_ref[...])
        kt = kc.T
        kt_ref[0, 2 * j] = kt[:HEAD_DIM].astype(BF16)
        kt_ref[0, 2 * j + 1] = kt[HEAD_DIM:].astype(BF16)
    ones_half = ((lane >= HEAD_DIM).astype(F32))[:1]
    for g in range(N_KV_HEADS):
        c0 = nq + N_KV_HEADS * HEAD_DIM + g * LANES
        v_ref[:, g * LANES:(g + 1) * LANES] = (y[:, c0:c0 + LANES] + ones_half).astype(BF16)


def _qkv_proj(x2, g, w, qg, kg, cos, sin, gsum, batch, seq):
    t = x2.shape[0]
    tm = ROW_TILE
    spb = seq // tm
    nkv = N_KV_HEADS * LANES
    return pl.pallas_call(
        _qkv_kernel,
        grid=(t // tm,),
        in_specs=[
            pl.BlockSpec((tm, D_MODEL), lambda i: (i, 0)),
            _const_spec((1, D_MODEL)),
            _const_spec(w.shape),
            _const_spec((1, LANES)),
            _const_spec((1, LANES)),
            pl.BlockSpec((tm, LANES), lambda i: (i % spb, 0)),
            pl.BlockSpec((tm, LANES), lambda i: (i % spb, 0)),
            _const_spec((LANES, LANES)),
        ],
        out_specs=[
            pl.BlockSpec((tm, N_HEADS * HEAD_DIM), lambda i: (i, 0)),
            pl.BlockSpec((1, N_KV_HEADS, HEAD_DIM, tm), lambda i: (i // spb, 0, 0, i % spb)),
            pl.BlockSpec((tm, nkv), lambda i: (i, 0)),
        ],
        out_shape=[
            jax.ShapeDtypeStruct((t, N_HEADS * HEAD_DIM), BF16),
            jax.ShapeDtypeStruct((batch, N_KV_HEADS, HEAD_DIM, seq), BF16),
            jax.ShapeDtypeStruct((t, nkv), BF16),
        ],
        compiler_params=pltpu.CompilerParams(
            dimension_semantics=("arbitrary",), vmem_limit_bytes=VMEM_LIMIT_BYTES),
        name="qkv_proj",
    )(x2, g, w, qg, kg, cos, sin, gsum)


def _attn_kernel(q_ref, kt_ref, v_ref, o_ref):
    tq = q_ref.shape[0]
    seq = v_ref.shape[0]
    q4 = jnp.concatenate(
        [q_ref[:, h * HEAD_DIM:(h + 1) * HEAD_DIM] for h in range(GROUP)], axis=0)
    m = None
    acc = None
    for c in range(seq // ATTN_TK):
        ks = slice(c * ATTN_TK, (c + 1) * ATTN_TK)
        s = jnp.dot(q4, kt_ref[0, 0, :, ks], preferred_element_type=F32)
        m_c = jnp.max(s, axis=-1, keepdims=True)
        m_new = m_c if m is None else jnp.maximum(m, m_c)
        p = jnp.exp2(s - m_new).astype(BF16)
        pv = jnp.dot(p, v_ref[ks, :], preferred_element_type=F32)
        acc = pv if acc is None else acc * jnp.exp2(m - m_new) + pv
        m = m_new
    out = acc * (1.0 / pltpu.roll(acc, HEAD_DIM, 1))
    lane = lax.broadcasted_iota(jnp.int32, (tq, LANES), 1)
    for hp in range(GROUP // 2):
        even = out[(2 * hp) * tq:(2 * hp + 1) * tq]
        odd = pltpu.roll(out[(2 * hp + 1) * tq:(2 * hp + 2) * tq], HEAD_DIM, 1)
        o_ref[:, hp * LANES:(hp + 1) * LANES] = jnp.where(lane < HEAD_DIM, even, odd).astype(BF16)


def _attention(q, kt, v, batch, seq):
    t = q.shape[0]
    tq = ATTN_TQ
    nq = seq // tq
    grp_w = GROUP * HEAD_DIM
    return pl.pallas_call(
        _attn_kernel,
        grid=(batch, N_KV_HEADS, nq),
        in_specs=[
            pl.BlockSpec((tq, grp_w), lambda b, g, i: (b * nq + i, g)),
            pl.BlockSpec((1, 1, HEAD_DIM, seq), lambda b, g, i: (b, g, 0, 0)),
            pl.BlockSpec((seq, LANES), lambda b, g, i: (b, g)),
        ],
        out_specs=pl.BlockSpec((tq, grp_w), lambda b, g, i: (b * nq + i, g)),
        out_shape=jax.ShapeDtypeStruct((t, N_HEADS * HEAD_DIM), BF16),
        compiler_params=pltpu.CompilerParams(
            dimension_semantics=("arbitrary", "arbitrary", "arbitrary"),
            vmem_limit_bytes=VMEM_LIMIT_BYTES),
        name="attention",
    )(q, kt, v)


def _oproj_mlp_kernel(h_ref, o_ref, wo_ref, g_ref, w1_ref, w2_ref, gf_ref, out_ref, *, final):
    h1 = h_ref[...] + jnp.dot(o_ref[...], wo_ref[...], preferred_element_type=F32)
    hn = _rms(h1, g_ref[...]).astype(BF16)
    acc = h1
    for c in range(D_FF // FF_TILE):
        sl = slice(c * FF_TILE, (c + 1) * FF_TILE)
        u = jnp.dot(hn, w1_ref[:, sl], preferred_element_type=F32)
        a = jnp.square(jnp.maximum(u, 0.0)).astype(BF16)
        acc = acc + jnp.dot(a, w2_ref[sl, :], preferred_element_type=F32)
    if final:
        acc = _rms(acc, gf_ref[...])
    out_ref[...] = acc


def _oproj_mlp(h, o, wo, g, w1, w2, gf, final):
    t = h.shape[0]
    tm = ROW_TILE
    return pl.pallas_call(
        functools.partial(_oproj_mlp_kernel, final=final),
        grid=(t // tm,),
        in_specs=[
            pl.BlockSpec((tm, D_MODEL), lambda i: (i, 0)),
            pl.BlockSpec((tm, o.shape[1]), lambda i: (i, 0)),
            _const_spec(wo.shape),
            _const_spec((1, D_MODEL)),
            _const_spec(w1.shape),
            _const_spec(w2.shape),
            _const_spec((1, D_MODEL)),
        ],
        out_specs=pl.BlockSpec((tm, D_MODEL), lambda i: (i, 0)),
        out_shape=jax.ShapeDtypeStruct((t, D_MODEL), F32),
        compiler_params=pltpu.CompilerParams(
            dimension_semantics=("arbitrary",), vmem_limit_bytes=VMEM_LIMIT_BYTES),
        name="oproj_mlp_final" if final else "oproj_mlp",
    )(h, o, wo, g, w1, w2, gf)


def _gla_in_kernel(x_ref, g_ref, w_ref, wz_ref, q_ref, k_ref, v_ref, r_ref, z_ref):
    xn = _rms(x_ref[...], g_ref[...]).astype(BF16)
    qk = jnp.dot(xn, w_ref[:, :2 * GLA_QK], preferred_element_type=F32)
    q_ref[...] = (qk[:, :GLA_QK] * (GLA_DK ** -0.5)).astype(BF16)
    k_ref[...] = qk[:, GLA_QK:].astype(BF16)
    v_ref[...] = jnp.dot(xn, w_ref[:, 2 * GLA_QK:2 * GLA_QK + GLA_V],
                         preferred_element_type=F32).astype(BF16)
    r_ref[...] = jnp.dot(xn, w_ref[:, 2 * GLA_QK + GLA_V:],
                         preferred_element_type=F32).astype(BF16)
    z_ref[...] = jnp.dot(xn, wz_ref[...], preferred_element_type=F32)


def _gla_inproj(h, g, w_main, w_z):
    t = h.shape[0]
    tm = ROW_TILE
    row = lambda i: (i, 0)
    return pl.pallas_call(
        _gla_in_kernel,
        grid=(t // tm,),
        in_specs=[
            pl.BlockSpec((tm, D_MODEL), row),
            _const_spec((1, D_MODEL)),
            _const_spec(w_main.shape),
            _const_spec(w_z.shape),
        ],
        out_specs=[
            pl.BlockSpec((tm, GLA_QK), row),
            pl.BlockSpec((tm, GLA_QK), row),
            pl.BlockSpec((tm, GLA_V), row),
            pl.BlockSpec((tm, GLA_V), row),
            pl.BlockSpec((tm, LANES), row),
        ],
        out_shape=[
            jax.ShapeDtypeStruct((t, GLA_QK), BF16),
            jax.ShapeDtypeStruct((t, GLA_QK), BF16),
            jax.ShapeDtypeStruct((t, GLA_V), BF16),
            jax.ShapeDtypeStruct((t, GLA_V), BF16),
            jax.ShapeDtypeStruct((t, LANES), F32),
        ],
        compiler_params=pltpu.CompilerParams(
            dimension_semantics=("arbitrary",), vmem_limit_bytes=VMEM_LIMIT_BYTES),
        name="gla_inproj",
    )(h, g, w_main, w_z)


def _gla_kernel(q_ref, k_ref, v_ref, r_ref, z_ref, wup_ref, bias_ref, og_ref, tri_ref, out_ref,
                qe_scr, kd_scr, dec_scr, o_scr, st_scr):
    seq = q_ref.shape[0]
    c64 = GLA_CHUNK
    n_chunks = seq // c64
    pre = GLA_PRE
    per = pre // c64

    ridx = lax.broadcasted_iota(jnp.int32, (pre, pre), 0)
    cidx = lax.broadcasted_iota(jnp.int32, (pre, pre), 1)
    same_chunk = (ridx // c64) == (cidx // c64)
    masks = (same_chunk & (ridx >= cidx), same_chunk & (ridx <= cidx))

    def precompute(i, carry):
        off = pl.multiple_of(i * pre, pre)
        rows = pl.ds(off, pre)
        zb = z_ref[rows, :]
        zb_hi = zb.astype(BF16)
        zb_lo = (zb - zb_hi.astype(F32)).astype(BF16)
        qf = q_ref[rows, :].astype(F32)
        kf = k_ref[rows, :].astype(F32)
        vv = v_ref[rows, :]
        wu = wup_ref[...]
        wu_hi = wu.astype(BF16)
        wu_lo = (wu - wu_hi.astype(F32)).astype(BF16)
        logit = (jnp.dot(zb_hi, wu_hi, preferred_element_type=F32)
                 + jnp.dot(zb_lo, wu_hi, preferred_element_type=F32)
                 + jnp.dot(zb_hi, wu_lo, preferred_element_type=F32)
                 + bias_ref[...])
        lg2 = (jnp.minimum(logit, 0.0) - jnp.log1p(jnp.exp(-jnp.abs(logit)))) * (1.0 / GATE_TAU)
        bs = [_split_dot(tri_ref[d], lg2[:, d * GLA_DK:(d + 1) * GLA_DK]) for d in range(2)]
        a_both = []
        for d in range(2):
            b = bs[d]
            b3 = b.reshape(per, c64, GLA_DK)
            tot3 = b3[:, c64 - 1:c64, :] if d == 0 else b3[:, 0:1, :]
            tot = jnp.broadcast_to(tot3, (per, c64, GLA_DK)).reshape(pre, GLA_DK)
            qe = (qf * jnp.exp(b)).astype(BF16)
            ke = (kf * jnp.exp(-b)).astype(BF16)
            qe_scr[d, rows, :] = qe
            kd_scr[d, rows, :] = (kf * jnp.exp(tot - b)).astype(BF16)
            a = lax.dot_general(qe, ke, (((1,), (1,)), ((), ())), preferred_element_type=F32)
            a_both.append(jnp.where(masks[d], a, 0.0).astype(BF16))
            for cc in range(per):
                dec_row = jnp.exp(tot3[cc])
                dec_scr[d, i * per + cc] = jnp.broadcast_to(dec_row, (GLA_DK, GLA_DK)).T
        o_scr[rows, :] = jnp.dot(jnp.concatenate(a_both, axis=1), jnp.concatenate([vv, vv], axis=0),
                                 preferred_element_type=F32)
        return carry

    lax.fori_loop(0, seq // pre, precompute, 0, unroll=GLA_PRE_UNROLL)

    st_scr[...] = jnp.zeros_like(st_scr)

    def scan(i, carry):
        for d in range(2):
            c = i if d == 0 else n_chunks - 1 - i
            off = pl.multiple_of(c * c64, c64)
            rows = pl.ds(off, c64)
            qe = qe_scr[d, rows, :]
            kd = kd_scr[d, rows, :]
            vv = v_ref[rows, :]
            st = st_scr[d]
            o_scr[rows, :] = o_scr[rows, :] + jnp.dot(qe, st.astype(BF16), preferred_element_type=F32)
            kv = lax.dot_general(kd, vv, (((0,), (0,)), ((), ())), preferred_element_type=F32)
            dec = dec_scr[d, c]
            st_scr[d] = jnp.concatenate([dec, dec], axis=1) * st + kv
        return carry

    lax.fori_loop(0, n_chunks, scan, 0, unroll=GLA_SCAN_UNROLL)

    def finish(i, carry):
        off = pl.multiple_of(i * pre, pre)
        rows = pl.ds(off, pre)
        on = _rms(o_scr[rows, :], og_ref[...])
        rr = r_ref[rows, :].astype(F32)
        out_ref[rows, :] = (on * (rr * jax.nn.sigmoid(rr))).astype(BF16)
        return carry

    lax.fori_loop(0, seq // pre, finish, 0)


def _gla_scan(q, k, v, r, z, wup, bias, og, tri, batch, seq):
    t = q.shape[0]
    n_chunks = seq // GLA_CHUNK
    bh = lambda b, h: (b, h)
    return pl.pallas_call(
        _gla_kernel,
        grid=(batch, GLA_HEADS),
        in_specs=[
            pl.BlockSpec((seq, GLA_DK), bh),
            pl.BlockSpec((seq, GLA_DK), bh),
            pl.BlockSpec((seq, GLA_DV), bh),
            pl.BlockSpec((seq, GLA_DV), bh),
            pl.BlockSpec((seq, LANES), lambda b, h: (b, 0)),
            pl.BlockSpec((LANES, 2 * GLA_DK), lambda b, h: (0, h)),
            pl.BlockSpec((1, 2 * GLA_DK), lambda b, h: (0, h)),
            _const_spec((1, GLA_DV)),
            _const_spec(tri.shape),
        ],
        out_specs=pl.BlockSpec((seq, GLA_DV), bh),
        out_shape=jax.ShapeDtypeStruct((t, GLA_V), BF16),
        scratch_shapes=[
            pltpu.VMEM((2, seq, GLA_DK), BF16),
            pltpu.VMEM((2, seq, GLA_DK), BF16),
            pltpu.VMEM((2, n_chunks, GLA_DK, LANES), F32),
            pltpu.VMEM((seq, GLA_DV), F32),
            pltpu.VMEM((2, GLA_DK, GLA_DV), F32),
        ],
        compiler_params=pltpu.CompilerParams(
            dimension_semantics=("arbitrary", "arbitrary"),
            vmem_limit_bytes=VMEM_LIMIT_BYTES),
        name="gla_scan",
    )(q, k, v, r, z, wup, bias, og, tri)


def _rope_tables(seq):
    pos = jnp.arange(seq, dtype=jnp.int32)
    half = HEAD_DIM // 2
    inv = ROPE_THETA ** (-jnp.arange(0, half, 2, dtype=F32) / half)
    ang_r = (pos // GRID_W).astype(F32)[:, None] * inv[None, :]
    ang_c = (pos % GRID_W).astype(F32)[:, None] * inv[None, :]
    cos_h = jnp.concatenate([jnp.cos(ang_r)] * 2 + [jnp.cos(ang_c)] * 2, axis=-1)
    sin_h = jnp.concatenate([-jnp.sin(ang_r), jnp.sin(ang_r), -jnp.sin(ang_c), jnp.sin(ang_c)], axis=-1)
    return jnp.tile(cos_h, (1, 2)), jnp.tile(sin_h, (1, 2))


def _gla_cumsum_matrices():
    n = GLA_PRE
    r = jnp.arange(n)[:, None]
    c = jnp.arange(n)[None, :]
    same = (r // GLA_CHUNK) == (c // GLA_CHUNK)
    return jnp.stack([same & (c <= r), same & (c >= r)]).astype(BF16)


def kernel(x, norm_mix, norm_mlp, attn_w_qkv, attn_q_norm, attn_k_norm, attn_w_o,
           gla_w_in, gla_w_gate_up, gla_b_gate, gla_out_norm, gla_w_o,
           mlp_w_in, mlp_w_out, final_norm):
    batch, seq, d = x.shape
    t = batch * seq
    h = x.reshape(t, d)
    row = lambda a: a.reshape(1, -1).astype(F32)

    cos, sin = _rope_tables(seq)
    gidx = jnp.arange(LANES)
    gsum = (gidx[:, None] // HEAD_DIM == gidx[None, :] // HEAD_DIM).astype(BF16)
    q_gain = jnp.tile(attn_q_norm[0], 2).reshape(1, LANES) * (LOG2E / math.sqrt(HEAD_DIM))
    k_gain = jnp.tile(attn_k_norm[0], 2).reshape(1, LANES)
    n_qk = (N_HEADS + N_KV_HEADS) * HEAD_DIM
    w_v = attn_w_qkv[0][:, n_qk:].reshape(d, N_KV_HEADS, HEAD_DIM)
    w_v = jnp.pad(w_v, ((0, 0), (0, 0), (0, LANES - HEAD_DIM))).reshape(d, N_KV_HEADS * LANES)
    w_qkv = jnp.concatenate([attn_w_qkv[0][:, :n_qk], w_v], axis=1).astype(BF16)
    q, kt, v = _qkv_proj(h, row(norm_mix[0]), w_qkv, q_gain, k_gain,
                         cos, sin, gsum, batch, seq)
    o = _attention(q, kt, v, batch, seq)
    h = _oproj_mlp(h, o, attn_w_o[0].astype(BF16), row(norm_mlp[0]),
                   mlp_w_in[0].astype(BF16), mlp_w_out[0].astype(BF16), row(final_norm), False)

    n_main = 2 * GLA_QK + 2 * GLA_V
    w_in = gla_w_in[0]
    w_main = w_in[:, :n_main].astype(BF16)
    w_z = jnp.pad(w_in[:, n_main:], ((0, 0), (0, LANES - 2 * GATE_RANK))).astype(BF16)
    gq, gk, gv, gr, gz = _gla_inproj(h, row(norm_mix[1]), w_main, w_z)
    wup = jnp.zeros((2, LANES, GLA_QK), F32)
    wup = wup.at[0, :GATE_RANK].set(gla_w_gate_up[0, 0])
    wup = wup.at[1, GATE_RANK:2 * GATE_RANK].set(gla_w_gate_up[0, 1])
    wup = wup.reshape(2, LANES, GLA_HEADS, GLA_DK).transpose(1, 2, 0, 3).reshape(LANES, 2 * GLA_QK)
    b_gate = gla_b_gate[0].reshape(2, GLA_HEADS, GLA_DK).transpose(1, 0, 2).reshape(1, 2 * GLA_QK)
    go = _gla_scan(gq, gk, gv, gr, gz, wup, b_gate,
                   row(gla_out_norm[0]), _gla_cumsum_matrices(), batch, seq)
    h = _oproj_mlp(h, go, gla_w_o[0].astype(BF16), row(norm_mlp[1]),
                   mlp_w_in[1].astype(BF16), mlp_w_out[1].astype(BF16), row(final_norm), True)
    return h.reshape(batch, seq, d)
```

```python
import functools
import math

import jax
import jax.numpy as jnp
from jax import lax
from jax.experimental import pallas as pl
from jax.experimental.pallas import tpu as pltpu

F32 = jnp.float32
BF16 = jnp.bfloat16

D_MODEL = 1024
GRID_W = 64
N_HEADS = 16
N_KV_HEADS = 4
HEAD_DIM = 64
GROUP = N_HEADS // N_KV_HEADS
ROPE_THETA = 10000.0
GLA_HEADS = 4
GLA_DK = 128
GLA_DV = 256
GLA_QK = GLA_HEADS * GLA_DK
GLA_V = GLA_HEADS * GLA_DV
GATE_RANK = 16
GATE_TAU = 16.0
D_FF = 4 * D_MODEL
EPS = 1e-6

LANES = 128
VMEM_LIMIT_BYTES = 56 * 1024 * 1024

ROW_TILE = 512
ATTN_TQ = 256
ATTN_TK = 512
FF_TILE = 1024
GLA_CHUNK = 64
GLA_TILE = 2 * GLA_CHUNK
GLA_GROUP = 4
LOG2E = 1.4426950408889634


def _const_spec(shape):
    nd = len(shape)
    return pl.BlockSpec(shape, lambda *_: (0,) * nd, pipeline_mode=pl.Buffered(1))


def _rms(x, gain):
    ms = jnp.mean(x * x, axis=-1, keepdims=True)
    return x * lax.rsqrt(ms + EPS) * gain


def _hi_lo(x_f32):
    hi = x_f32.astype(BF16)
    lo = (x_f32 - hi.astype(F32)).astype(BF16)
    return hi, lo


def _qkv_kernel(x_ref, g_ref, w_ref, qg_ref, kg_ref, cos_ref, sin_ref, gsum_ref,
                q_ref, kt_ref, v_ref):
    tm = x_ref.shape[0]
    xn = _rms(x_ref[...], g_ref[...]).astype(BF16)
    y = jnp.dot(xn, w_ref[...], preferred_element_type=F32)
    cos = cos_ref[...]
    sin = sin_ref[...]
    lane = lax.broadcasted_iota(jnp.int32, (tm, LANES), 1)
    first_half = (lane % 32) < 16
    gsum = gsum_ref[...]

    def norm_rope(c, gain):
        ss = jnp.dot((c * c).astype(BF16), gsum, preferred_element_type=F32)
        cn = c * lax.rsqrt(ss * (1.0 / HEAD_DIM) + EPS) * gain
        rot = jnp.where(first_half, pltpu.roll(cn, LANES - 16, 1), pltpu.roll(cn, 16, 1))
        return cn * cos + rot * sin

    nq = N_HEADS * HEAD_DIM
    for j in range(nq // LANES):
        sl = slice(j * LANES, (j + 1) * LANES)
        q_ref[:, sl] = norm_rope(y[:, sl], qg_ref[...]).astype(BF16)
    for j in range(N_KV_HEADS * HEAD_DIM // LANES):
        kc = norm_rope(y[:, nq + j * LANES: nq + (j + 1) * LANES], kg_ref[...])
        kt = kc.T
        kt_ref[0, 2 * j] = kt[:HEAD_DIM].astype(BF16)
        kt_ref[0, 2 * j + 1] = kt[HEAD_DIM:].astype(BF16)
    ones_half = ((lane >= HEAD_DIM).astype(F32))[:1]
    for g in range(N_KV_HEADS):
        c0 = nq + N_KV_HEADS * HEAD_DIM + g * LANES
        v_ref[:, g * LANES:(g + 1) * LANES] = (y[:, c0:c0 + LANES] + ones_half).astype(BF16)


def _qkv_proj(x2, g, w, qg, kg, cos, sin, gsum, batch, seq):
    t = x2.shape[0]
    tm = ROW_TILE
    spb = seq // tm
    nkv = N_KV_HEADS * LANES
    return pl.pallas_call(
        _qkv_kernel,
        grid=(t // tm,),
        in_specs=[
            pl.BlockSpec((tm, D_MODEL), lambda i: (i, 0)),
            _const_spec((1, D_MODEL)),
            _const_spec(w.shape),
            _const_spec((1, LANES)),
            _const_spec((1, LANES)),
            pl.BlockSpec((tm, LANES), lambda i: (i % spb, 0)),
            pl.BlockSpec((tm, LANES), lambda i: (i % spb, 0)),
            _const_spec((LANES, LANES)),
        ],
        out_specs=[
            pl.BlockSpec((tm, N_HEADS * HEAD_DIM), lambda i: (i, 0)),
            pl.BlockSpec((1, N_KV_HEADS, HEAD_DIM, tm), lambda i: (i // spb, 0, 0, i % spb)),
            pl.BlockSpec((tm, nkv), lambda i: (i, 0)),
        ],
        out_shape=[
            jax.ShapeDtypeStruct((t, N_HEADS * HEAD_DIM), BF16),
            jax.ShapeDtypeStruct((batch, N_KV_HEADS, HEAD_DIM, seq), BF16),
            jax.ShapeDtypeStruct((t, nkv), BF16),
        ],
        compiler_params=pltpu.CompilerParams(
            dimension_semantics=("arbitrary",), vmem_limit_bytes=VMEM_LIMIT_BYTES),
        name="qkv_proj",
    )(x2, g, w, qg, kg, cos, sin, gsum)


def _attn_kernel(q_ref, kt_ref, v_ref, o_ref):
    tq = q_ref.shape[0]
    seq = v_ref.shape[0]
    q4 = jnp.concatenate(
        [q_ref[:, h * HEAD_DIM:(h + 1) * HEAD_DIM] for h in range(GROUP)], axis=0)
    m = None
    acc = None
    for c in range(seq // ATTN_TK):
        ks = slice(c * ATTN_TK, (c + 1) * ATTN_TK)
        s = jnp.dot(q4, kt_ref[0, 0, :, ks], preferred_element_type=F32)
        m_c = jnp.max(s, axis=-1, keepdims=True)
        m_new = m_c if m is None else jnp.maximum(m, m_c)
        p = jnp.exp2(s - m_new).astype(BF16)
        pv = jnp.dot(p, v_ref[ks, :], preferred_element_type=F32)
        acc = pv if acc is None else acc * jnp.exp2(m - m_new) + pv
        m = m_new
    out = acc * (1.0 / pltpu.roll(acc, HEAD_DIM, 1))
    lane = lax.broadcasted_iota(jnp.int32, (tq, LANES), 1)
    for hp in range(GROUP // 2):
        even = out[(2 * hp) * tq:(2 * hp + 1) * tq]
        odd = pltpu.roll(out[(2 * hp + 1) * tq:(2 * hp + 2) * tq], HEAD_DIM, 1)
        o_ref[:, hp * LANES:(hp + 1) * LANES] = jnp.where(lane < HEAD_DIM, even, odd).astype(BF16)


def _attention(q, kt, v, batch, seq):
    t = q.shape[0]
    tq = ATTN_TQ
    nq = seq // tq
    grp_w = GROUP * HEAD_DIM
    return pl.pallas_call(
        _attn_kernel,
        grid=(batch, N_KV_HEADS, nq),
        in_specs=[
            pl.BlockSpec((tq, grp_w), lambda b, g, i: (b * nq + i, g)),
            pl.BlockSpec((1, 1, HEAD_DIM, seq), lambda b, g, i: (b, g, 0, 0)),
            pl.BlockSpec((seq, LANES), lambda b, g, i: (b, g)),
        ],
        out_specs=pl.BlockSpec((tq, grp_w), lambda b, g, i: (b * nq + i, g)),
        out_shape=jax.ShapeDtypeStruct((t, N_HEADS * HEAD_DIM), BF16),
        compiler_params=pltpu.CompilerParams(
            dimension_semantics=("arbitrary", "arbitrary", "arbitrary"),
            vmem_limit_bytes=VMEM_LIMIT_BYTES),
        name="attention",
    )(q, kt, v)


def _oproj_mlp_kernel(h_ref, o_ref, wo_ref, g_ref, w1_ref, w2_ref, gf_ref, out_ref, *, final):
    h1 = h_ref[...] + jnp.dot(o_ref[...], wo_ref[...], preferred_element_type=F32)
    hn = _rms(h1, g_ref[...]).astype(BF16)
    acc = h1
    for c in range(D_FF // FF_TILE):
        sl = slice(c * FF_TILE, (c + 1) * FF_TILE)
        u = jnp.dot(hn, w1_ref[:, sl], preferred_element_type=F32)
        a = jnp.square(jnp.maximum(u, 0.0)).astype(BF16)
        acc = acc + jnp.dot(a, w2_ref[sl, :], preferred_element_type=F32)
    if final:
        acc = _rms(acc, gf_ref[...])
    out_ref[...] = acc


def _oproj_mlp(h, o, wo, g, w1, w2, gf, final):
    t = h.shape[0]
    tm = ROW_TILE
    return pl.pallas_call(
        functools.partial(_oproj_mlp_kernel, final=final),
        grid=(t // tm,),
        in_specs=[
            pl.BlockSpec((tm, D_MODEL), lambda i: (i, 0)),
            pl.BlockSpec((tm, o.shape[1]), lambda i: (i, 0)),
            _const_spec(wo.shape),
            _const_spec((1, D_MODEL)),
            _const_spec(w1.shape),
            _const_spec(w2.shape),
            _const_spec((1, D_MODEL)),
        ],
        out_specs=pl.BlockSpec((tm, D_MODEL), lambda i: (i, 0)),
        out_shape=jax.ShapeDtypeStruct((t, D_MODEL), F32),
        compiler_params=pltpu.CompilerParams(
            dimension_semantics=("arbitrary",), vmem_limit_bytes=VMEM_LIMIT_BYTES),
        name="oproj_mlp_final" if final else "oproj_mlp",
    )(h, o, wo, g, w1, w2, gf)


def _gla_in_kernel(x_ref, g_ref, w_ref, wz_ref, q_ref, k_ref, v_ref, r_ref, z_ref):
    xn = _rms(x_ref[...], g_ref[...]).astype(BF16)
    qk = jnp.dot(xn, w_ref[:, :2 * GLA_QK], preferred_element_type=F32)
    q_ref[...] = (qk[:, :GLA_QK] * (GLA_DK ** -0.5)).astype(BF16)
    k_ref[...] = qk[:, GLA_QK:].astype(BF16)
    v_ref[...] = jnp.dot(xn, w_ref[:, 2 * GLA_QK:2 * GLA_QK + GLA_V],
                         preferred_element_type=F32).astype(BF16)
    r_ref[...] = jnp.dot(xn, w_ref[:, 2 * GLA_QK + GLA_V:],
                         preferred_element_type=F32).astype(BF16)
    z_ref[...] = jnp.dot(xn, wz_ref[...], preferred_element_type=F32)


def _gla_inproj(h, g, w_main, w_z):
    t = h.shape[0]
    tm = ROW_TILE
    row = lambda i: (i, 0)
    return pl.pallas_call(
        _gla_in_kernel,
        grid=(t // tm,),
        in_specs=[
            pl.BlockSpec((tm, D_MODEL), row),
            _const_spec((1, D_MODEL)),
            _const_spec(w_main.shape),
            _const_spec(w_z.shape),
        ],
        out_specs=[
            pl.BlockSpec((tm, GLA_QK), row),
            pl.BlockSpec((tm, GLA_QK), row),
            pl.BlockSpec((tm, GLA_V), row),
            pl.BlockSpec((tm, GLA_V), row),
            pl.BlockSpec((tm, LANES), row),
        ],
        out_shape=[
            jax.ShapeDtypeStruct((t, GLA_QK), BF16),
            jax.ShapeDtypeStruct((t, GLA_QK), BF16),
            jax.ShapeDtypeStruct((t, GLA_V), BF16),
            jax.ShapeDtypeStruct((t, GLA_V), BF16),
            jax.ShapeDtypeStruct((t, LANES), F32),
        ],
        compiler_params=pltpu.CompilerParams(
            dimension_semantics=("arbitrary",), vmem_limit_bytes=VMEM_LIMIT_BYTES),
        name="gla_inproj",
    )(h, g, w_main, w_z)


def _gla_kernel(q_ref, k_ref, v_ref, r_ref, z_ref, wcat_ref, bias_ref, og_ref, cum_ref, out_ref,
                lhs_scr, kv_scr, dec_scr):
    seq = q_ref.shape[0]
    c64 = GLA_CHUNK
    tile = GLA_TILE
    n_tiles = seq // tile
    grp = GLA_GROUP
    grp_rows = grp * tile
    dk = GLA_DK

    ridx = lax.broadcasted_iota(jnp.int32, (tile, tile), 0)
    cidx = lax.broadcasted_iota(jnp.int32, (tile, tile), 1)
    same_chunk = (ridx // c64) == (cidx // c64)
    fwd_diag = same_chunk & (ridx >= cidx)
    fwd_cross = (ridx >= c64) & (cidx < c64)
    bwd_diag = same_chunk & (ridx <= cidx)
    bwd_cross = (ridx < c64) & (cidx >= c64)
    second = lax.broadcasted_iota(jnp.int32, (tile, dk), 0) >= c64
    zeros_td = jnp.zeros((tile, dk), BF16)
    zeros_2td = jnp.zeros((2 * tile, dk), BF16)

    def phase_a(g, carry):
        goff = pl.multiple_of(g * grp_rows, grp_rows)
        z_hi, z_lo = _hi_lo(z_ref[pl.ds(goff, grp_rows), :])
        zcat = jnp.concatenate([z_hi, z_lo, z_hi], axis=1)
        logit = jnp.dot(zcat, wcat_ref[...], preferred_element_type=F32) + bias_ref[...]
        lg2 = (jnp.minimum(logit, 0.0) - jnp.log1p(jnp.exp(-jnp.abs(logit)))) * (1.0 / GATE_TAU)

        sums = []
        for t in range(grp):
            lg_hi, lg_lo = _hi_lo(lg2[t * tile:(t + 1) * tile])
            rhs = jnp.concatenate([
                jnp.concatenate([lg_hi[:, :dk], zeros_td], axis=1),
                jnp.concatenate([lg_lo[:, :dk], zeros_td], axis=1),
                jnp.concatenate([zeros_td, lg_hi[:, dk:]], axis=1),
                jnp.concatenate([zeros_td, lg_lo[:, dk:]], axis=1)], axis=0)
            sums.append(jnp.dot(cum_ref[...], rhs, preferred_element_type=F32))

        scores = []
        kvs = []
        for t in range(grp):
            rows = pl.ds(goff + t * tile, tile)
            qf = q_ref[rows, :].astype(F32)
            kf = k_ref[rows, :].astype(F32)
            vv = v_ref[rows, :]
            b_f = sums[t][:, :dk]
            b_b = sums[t][:, dk:]
            tf0, tf1 = b_f[c64 - 1:c64], b_f[tile - 1:tile]
            tb0, tb1 = b_b[0:1], b_b[c64:c64 + 1]
            qe_f = (qf * jnp.exp(b_f)).astype(BF16)
            ke_f = (kf * jnp.exp(-b_f)).astype(BF16)
            kd_f32 = kf * jnp.exp(jnp.where(second, tf1, tf0) - b_f)
            qe_b = (qf * jnp.exp(b_b)).astype(BF16)
            ke_b = (kf * jnp.exp(-b_b)).astype(BF16)
            kd_b32 = kf * jnp.exp(jnp.where(second, tb1, tb0) - b_b)
            qt_f = (qf * jnp.exp(b_f) * jnp.where(second, jnp.exp(tf0), 1.0)).astype(BF16)
            qt_b = (qf * jnp.exp(b_b) * jnp.where(second, 1.0, jnp.exp(tb1))).astype(BF16)
            kt_f = kd_f32 * jnp.where(second, 1.0, jnp.exp(tf1))
            kt_b = kd_b32 * jnp.where(second, jnp.exp(tb0), 1.0)

            keys = jnp.concatenate([
                jnp.concatenate([ke_f, kd_f32.astype(BF16)], axis=0),
                jnp.concatenate([ke_b, kd_b32.astype(BF16)], axis=0)], axis=0)
            keys = jnp.concatenate([
                jnp.concatenate([keys[:2 * tile], zeros_2td], axis=1),
                jnp.concatenate([zeros_2td, keys[2 * tile:]], axis=1)], axis=0)
            sc = lax.dot_general(jnp.concatenate([qe_f, qe_b], axis=1), keys,
                                 (((1,), (1,)), ((), ())), preferred_element_type=F32)
            scores.append((sc, qt_f, qt_b))

            kt_both = jnp.concatenate([
                jnp.concatenate([kt_f.T.astype(BF16), zeros_td], axis=1),
                jnp.concatenate([zeros_td, kt_b.T.astype(BF16)], axis=1)], axis=0)
            kvs.append(jnp.dot(kt_both, jnp.concatenate([vv, vv], axis=0),
                               preferred_element_type=F32))
            idx = g * grp + t
            dec_scr[0, idx] = jnp.broadcast_to(jnp.exp(tf0 + tf1), (dk, dk)).T
            dec_scr[1, idx] = jnp.broadcast_to(jnp.exp(tb0 + tb1), (dk, dk)).T

        for t in range(grp):
            rows = pl.ds(goff + t * tile, tile)
            sc, qt_f, qt_b = scores[t]
            a_sum = (jnp.where(fwd_diag, sc[:, :tile], 0.0)
                     + jnp.where(fwd_cross, sc[:, tile:2 * tile], 0.0)
                     + jnp.where(bwd_diag, sc[:, 2 * tile:3 * tile], 0.0)
                     + jnp.where(bwd_cross, sc[:, 3 * tile:], 0.0))
            lhs_scr[rows, :] = jnp.concatenate([a_sum.astype(BF16), qt_f, qt_b], axis=1)
            idx = g * grp + t
            kv_scr[0, idx] = kvs[t][:dk]
            kv_scr[1, idx] = kvs[t][dk:]
        return carry

    lax.fori_loop(0, n_tiles // grp, phase_a, 0)

    for d in range(2):
        def step(i, run, d=d):
            t = i if d == 0 else n_tiles - 1 - i
            inc = kv_scr[d, t]
            kv_scr[d, t] = run
            dec = dec_scr[d, t]
            return jnp.concatenate([dec, dec], axis=1) * run + inc

        lax.fori_loop(0, n_tiles, step, jnp.zeros((dk, GLA_DV), F32))

    def phase_c(g, carry):
        goff = pl.multiple_of(g * grp_rows, grp_rows)
        outs = []
        for t in range(grp):
            rows = pl.ds(goff + t * tile, tile)
            idx = g * grp + t
            rhs = jnp.concatenate([v_ref[rows, :], kv_scr[0, idx].astype(BF16),
                                   kv_scr[1, idx].astype(BF16)], axis=0)
            outs.append(jnp.dot(lhs_scr[rows, :], rhs, preferred_element_type=F32))
        for t in range(grp):
            rows = pl.ds(goff + t * tile, tile)
            on = _rms(outs[t], og_ref[...])
            rr = r_ref[rows, :].astype(F32)
            out_ref[rows, :] = (on * (rr * jax.nn.sigmoid(rr))).astype(BF16)
        return carry

    lax.fori_loop(0, n_tiles // grp, phase_c, 0)


def _gla_scan(q, k, v, r, z, wcat, bias, og, cum, batch, seq):
    t = q.shape[0]
    n_tiles = seq // GLA_TILE
    bh = lambda b, h: (b, h)
    return pl.pallas_call(
        _gla_kernel,
        grid=(batch, GLA_HEADS),
        in_specs=[
            pl.BlockSpec((seq, GLA_DK), bh),
            pl.BlockSpec((seq, GLA_DK), bh),
            pl.BlockSpec((seq, GLA_DV), bh),
            pl.BlockSpec((seq, GLA_DV), bh),
            pl.BlockSpec((seq, LANES), lambda b, h: (b, 0)),
            pl.BlockSpec((3 * LANES, 2 * GLA_DK), lambda b, h: (0, h)),
            pl.BlockSpec((1, 2 * GLA_DK), lambda b, h: (0, h)),
            _const_spec((1, GLA_DV)),
            _const_spec(cum.shape),
        ],
        out_specs=pl.BlockSpec((seq, GLA_DV), bh),
        out_shape=jax.ShapeDtypeStruct((t, GLA_V), BF16),
        scratch_shapes=[
            pltpu.VMEM((seq, GLA_TILE + 2 * GLA_DK), BF16),
            pltpu.VMEM((2, n_tiles, GLA_DK, GLA_DV), F32),
            pltpu.VMEM((2, n_tiles, GLA_DK, LANES), F32),
        ],
        compiler_params=pltpu.CompilerParams(
            dimension_semantics=("arbitrary", "arbitrary"),
            vmem_limit_bytes=VMEM_LIMIT_BYTES),
        name="gla_scan",
    )(q, k, v, r, z, wcat, bias, og, cum)


def _rope_tables(seq):
    pos = jnp.arange(seq, dtype=jnp.int32)
    half = HEAD_DIM // 2
    inv = ROPE_THETA ** (-jnp.arange(0, half, 2, dtype=F32) / half)
    ang_r = (pos // GRID_W).astype(F32)[:, None] * inv[None, :]
    ang_c = (pos % GRID_W).astype(F32)[:, None] * inv[None, :]
    cos_h = jnp.concatenate([jnp.cos(ang_r)] * 2 + [jnp.cos(ang_c)] * 2, axis=-1)
    sin_h = jnp.concatenate([-jnp.sin(ang_r), jnp.sin(ang_r), -jnp.sin(ang_c), jnp.sin(ang_c)], axis=-1)
    return jnp.tile(cos_h, (1, 2)), jnp.tile(sin_h, (1, 2))


def _gla_cumsum_matrix():
    n = GLA_TILE
    r = jnp.arange(n)[:, None]
    c = jnp.arange(n)[None, :]
    same = (r // GLA_CHUNK) == (c // GLA_CHUNK)
    prefix = (same & (c <= r)).astype(BF16)
    suffix = (same & (c >= r)).astype(BF16)
    return jnp.concatenate([prefix, prefix, suffix, suffix], axis=1)


def kernel(x, norm_mix, norm_mlp, attn_w_qkv, attn_q_norm, attn_k_norm, attn_w_o,
           gla_w_in, gla_w_gate_up, gla_b_gate, gla_out_norm, gla_w_o,
           mlp_w_in, mlp_w_out, final_norm):
    batch, seq, d = x.shape
    t = batch * seq
    h = x.reshape(t, d)
    row = lambda a: a.reshape(1, -1).astype(F32)

    cos, sin = _rope_tables(seq)
    gidx = jnp.arange(LANES)
    gsum = (gidx[:, None] // HEAD_DIM == gidx[None, :] // HEAD_DIM).astype(BF16)
    q_gain = jnp.tile(attn_q_norm[0], 2).reshape(1, LANES) * (LOG2E / math.sqrt(HEAD_DIM))
    k_gain = jnp.tile(attn_k_norm[0], 2).reshape(1, LANES)
    n_qk = (N_HEADS + N_KV_HEADS) * HEAD_DIM
    w_v = attn_w_qkv[0][:, n_qk:].reshape(d, N_KV_HEADS, HEAD_DIM)
    w_v = jnp.pad(w_v, ((0, 0), (0, 0), (0, LANES - HEAD_DIM))).reshape(d, N_KV_HEADS * LANES)
    w_qkv = jnp.concatenate([attn_w_qkv[0][:, :n_qk], w_v], axis=1).astype(BF16)
    q, kt, v = _qkv_proj(h, row(norm_mix[0]), w_qkv, q_gain, k_gain,
                         cos, sin, gsum, batch, seq)
    o = _attention(q, kt, v, batch, seq)
    h = _oproj_mlp(h, o, attn_w_o[0].astype(BF16), row(norm_mlp[0]),
                   mlp_w_in[0].astype(BF16), mlp_w_out[0].astype(BF16), row(final_norm), False)

    n_main = 2 * GLA_QK + 2 * GLA_V
    w_in = gla_w_in[0]
    w_main = w_in[:, :n_main].astype(BF16)
    w_z = jnp.pad(w_in[:, n_main:], ((0, 0), (0, LANES - 2 * GATE_RANK))).astype(BF16)
    gq, gk, gv, gr, gz = _gla_inproj(h, row(norm_mix[1]), w_main, w_z)
    wup = jnp.zeros((2, LANES, GLA_QK), F32)
    wup = wup.at[0, :GATE_RANK].set(gla_w_gate_up[0, 0])
    wup = wup.at[1, GATE_RANK:2 * GATE_RANK].set(gla_w_gate_up[0, 1])
    wup = wup.reshape(2, LANES, GLA_HEADS, GLA_DK).transpose(1, 2, 0, 3).reshape(LANES, 2 * GLA_QK)
    wup_hi, wup_lo = _hi_lo(wup)
    wcat = jnp.concatenate([wup_hi, wup_hi, wup_lo], axis=0)
    b_gate = gla_b_gate[0].reshape(2, GLA_HEADS, GLA_DK).transpose(1, 0, 2).reshape(1, 2 * GLA_QK)
    go = _gla_scan(gq, gk, gv, gr, gz, wcat, b_gate,
                   row(gla_out_norm[0]), _gla_cumsum_matrix(), batch, seq)
    h = _oproj_mlp(h, go, gla_w_o[0].astype(BF16), row(norm_mlp[1]),
                   mlp_w_in[1].astype(BF16), mlp_w_out[1].astype(BF16), row(final_norm), True)
    return h.reshape(batch, seq, d)
```

```python
import functools
import math

import jax
import jax.numpy as jnp
from jax import lax
from jax.experimental import pallas as pl
from jax.experimental.pallas import tpu as pltpu

F32 = jnp.float32
BF16 = jnp.bfloat16

D_MODEL = 1024
GRID_W = 64
N_HEADS = 16
N_KV_HEADS = 4
HEAD_DIM = 64
GROUP = N_HEADS // N_KV_HEADS
ROPE_THETA = 10000.0
GLA_HEADS = 4
GLA_DK = 128
GLA_DV = 256
GLA_QK = GLA_HEADS * GLA_DK
GLA_V = GLA_HEADS * GLA_DV
GATE_RANK = 16
GATE_TAU = 16.0
D_FF = 4 * D_MODEL
EPS = 1e-6

LANES = 128
VMEM_LIMIT_BYTES = 56 * 1024 * 1024

ROW_TILE = 512
ATTN_TQ = 512
ATTN_TK = 512
FF_TILE = 1024
GLA_CHUNK = 64
GLA_TILE = 2 * GLA_CHUNK
GLA_GROUP = 4
LOG2E = 1.4426950408889634


def _const_spec(shape):
    nd = len(shape)
    return pl.BlockSpec(shape, lambda *_: (0,) * nd, pipeline_mode=pl.Buffered(1))


def _rms(x, gain):
    ms = jnp.mean(x * x, axis=-1, keepdims=True)
    return x * lax.rsqrt(ms + EPS) * gain


def _hi_lo(x_f32):
    hi = x_f32.astype(BF16)
    lo = (x_f32 - hi.astype(F32)).astype(BF16)
    return hi, lo


def _qkv_kernel(x_ref, g_ref, w_ref, qg_ref, kg_ref, cos_ref, sin_ref, gsum_ref,
                q_ref, kt_ref, v_ref):
    tm = x_ref.shape[0]
    xn = _rms(x_ref[...], g_ref[...]).astype(BF16)
    cos = cos_ref[...]
    sin = sin_ref[...]
    lane = lax.broadcasted_iota(jnp.int32, (tm, LANES), 1)
    first_half = (lane % 32) < 16
    gsum = gsum_ref[...]

    def norm_rope(c, gain):
        ss = jnp.dot((c * c).astype(BF16), gsum, preferred_element_type=F32)
        cn = c * lax.rsqrt(ss * (1.0 / HEAD_DIM) + EPS) * gain
        rot = jnp.where(first_half, pltpu.roll(cn, LANES - 16, 1), pltpu.roll(cn, 16, 1))
        return cn * cos + rot * sin

    y = jnp.dot(xn, w_ref[...], preferred_element_type=F32)
    nq = N_HEADS * HEAD_DIM
    nk = N_KV_HEADS * HEAD_DIM
    for j in range(nq // LANES):
        sl = slice(j * LANES, (j + 1) * LANES)
        q_ref[:, sl] = norm_rope(y[:, sl], qg_ref[...]).astype(BF16)
    for j in range(nk // LANES):
        kt = norm_rope(y[:, nq + j * LANES: nq + (j + 1) * LANES], kg_ref[...]).T
        kt_ref[0, 2 * j] = kt[:HEAD_DIM].astype(BF16)
        kt_ref[0, 2 * j + 1] = kt[HEAD_DIM:].astype(BF16)
    for g in range(N_KV_HEADS):
        c0 = nq + nk + (g // 2) * LANES
        pair = y[:, c0:c0 + LANES]
        if g % 2:
            pair = pltpu.roll(pair, HEAD_DIM, 1)
        v_ref[:, g * LANES:(g + 1) * LANES] = jnp.where(lane < HEAD_DIM, pair, 1.0).astype(BF16)


def _qkv_proj(x2, g, w, qg, kg, cos, sin, gsum, batch, seq):
    t = x2.shape[0]
    tm = ROW_TILE
    spb = seq // tm
    nkv = N_KV_HEADS * LANES
    return pl.pallas_call(
        _qkv_kernel,
        grid=(t // tm,),
        in_specs=[
            pl.BlockSpec((tm, D_MODEL), lambda i: (i, 0)),
            _const_spec((1, D_MODEL)),
            _const_spec(w.shape),
            _const_spec((1, LANES)),
            _const_spec((1, LANES)),
            pl.BlockSpec((tm, LANES), lambda i: (i % spb, 0)),
            pl.BlockSpec((tm, LANES), lambda i: (i % spb, 0)),
            _const_spec((LANES, LANES)),
        ],
        out_specs=[
            pl.BlockSpec((tm, N_HEADS * HEAD_DIM), lambda i: (i, 0)),
            pl.BlockSpec((1, N_KV_HEADS, HEAD_DIM, tm), lambda i: (i // spb, 0, 0, i % spb)),
            pl.BlockSpec((tm, nkv), lambda i: (i, 0)),
        ],
        out_shape=[
            jax.ShapeDtypeStruct((t, N_HEADS * HEAD_DIM), BF16),
            jax.ShapeDtypeStruct((batch, N_KV_HEADS, HEAD_DIM, seq), BF16),
            jax.ShapeDtypeStruct((t, nkv), BF16),
        ],
        compiler_params=pltpu.CompilerParams(
            dimension_semantics=("arbitrary",), vmem_limit_bytes=VMEM_LIMIT_BYTES),
        name="qkv_proj",
    )(x2, g, w, qg, kg, cos, sin, gsum)


def _attn_kernel(q_ref, kt_ref, v_ref, o_ref):
    tq = q_ref.shape[0]
    seq = v_ref.shape[0]
    q4 = jnp.concatenate(
        [q_ref[:, h * HEAD_DIM:(h + 1) * HEAD_DIM] for h in range(GROUP)], axis=0)
    m = None
    acc = None
    for c in range(seq // ATTN_TK):
        ks = slice(c * ATTN_TK, (c + 1) * ATTN_TK)
        s = jnp.dot(q4, kt_ref[0, 0, :, ks], preferred_element_type=F32)
        m_c = jnp.max(s, axis=-1, keepdims=True)
        m_new = m_c if m is None else jnp.maximum(m, m_c)
        p = jnp.exp2((s - m_new).astype(BF16))
        pv = jnp.dot(p, v_ref[ks, :], preferred_element_type=F32)
        acc = pv if acc is None else acc * jnp.exp2(m - m_new) + pv
        m = m_new
    out = acc * (1.0 / pltpu.roll(acc, HEAD_DIM, 1))
    lane = lax.broadcasted_iota(jnp.int32, (tq, LANES), 1)
    for hp in range(GROUP // 2):
        even = out[(2 * hp) * tq:(2 * hp + 1) * tq]
        odd = pltpu.roll(out[(2 * hp + 1) * tq:(2 * hp + 2) * tq], HEAD_DIM, 1)
        o_ref[:, hp * LANES:(hp + 1) * LANES] = jnp.where(lane < HEAD_DIM, even, odd).astype(BF16)


def _attention(q, kt, v, batch, seq):
    t = q.shape[0]
    tq = ATTN_TQ
    nq = seq // tq
    grp_w = GROUP * HEAD_DIM
    return pl.pallas_call(
        _attn_kernel,
        grid=(batch, N_KV_HEADS, nq),
        in_specs=[
            pl.BlockSpec((tq, grp_w), lambda b, g, i: (b * nq + i, g)),
            pl.BlockSpec((1, 1, HEAD_DIM, seq), lambda b, g, i: (b, g, 0, 0)),
            pl.BlockSpec((seq, LANES), lambda b, g, i: (b, g)),
        ],
        out_specs=pl.BlockSpec((tq, grp_w), lambda b, g, i: (b * nq + i, g)),
        out_shape=jax.ShapeDtypeStruct((t, N_HEADS * HEAD_DIM), BF16),
        compiler_params=pltpu.CompilerParams(
            dimension_semantics=("arbitrary", "arbitrary", "arbitrary"),
            vmem_limit_bytes=VMEM_LIMIT_BYTES),
        name="attention",
    )(q, kt, v)


def _oproj_mlp_kernel(h_ref, o_ref, wo_ref, g_ref, w1_ref, w2_ref, gf_ref, out_ref, *, final):
    h1 = h_ref[...] + jnp.dot(o_ref[...], wo_ref[...], preferred_element_type=F32)
    hn = _rms(h1, g_ref[...]).astype(BF16)
    acc = h1
    for c in range(D_FF // FF_TILE):
        sl = slice(c * FF_TILE, (c + 1) * FF_TILE)
        u = jnp.dot(hn, w1_ref[:, sl], preferred_element_type=F32)
        a = jnp.square(jnp.maximum(u, 0.0)).astype(BF16)
        acc = acc + jnp.dot(a, w2_ref[sl, :], preferred_element_type=F32)
    if final:
        acc = _rms(acc, gf_ref[...])
    out_ref[...] = acc


def _oproj_mlp(h, o, wo, g, w1, w2, gf, final):
    t = h.shape[0]
    tm = ROW_TILE
    return pl.pallas_call(
        functools.partial(_oproj_mlp_kernel, final=final),
        grid=(t // tm,),
        in_specs=[
            pl.BlockSpec((tm, D_MODEL), lambda i: (i, 0)),
            pl.BlockSpec((tm, o.shape[1]), lambda i: (i, 0)),
            _const_spec(wo.shape),
            _const_spec((1, D_MODEL)),
            _const_spec(w1.shape),
            _const_spec(w2.shape),
            _const_spec((1, D_MODEL)),
        ],
        out_specs=pl.BlockSpec((tm, D_MODEL), lambda i: (i, 0)),
        out_shape=jax.ShapeDtypeStruct((t, D_MODEL), F32),
        compiler_params=pltpu.CompilerParams(
            dimension_semantics=("arbitrary",), vmem_limit_bytes=VMEM_LIMIT_BYTES),
        name="oproj_mlp_final" if final else "oproj_mlp",
    )(h, o, wo, g, w1, w2, gf)


def _gla_in_kernel(x_ref, g_ref, w_ref, wz_ref, q_ref, k_ref, v_ref, r_ref, z_ref):
    xn = _rms(x_ref[...], g_ref[...]).astype(BF16)
    qk = jnp.dot(xn, w_ref[:, :2 * GLA_QK], preferred_element_type=F32)
    q_ref[...] = (qk[:, :GLA_QK] * (GLA_DK ** -0.5)).astype(BF16)
    k_ref[...] = qk[:, GLA_QK:].astype(BF16)
    v_ref[...] = jnp.dot(xn, w_ref[:, 2 * GLA_QK:2 * GLA_QK + GLA_V],
                         preferred_element_type=F32).astype(BF16)
    r_ref[...] = jnp.dot(xn, w_ref[:, 2 * GLA_QK + GLA_V:],
                         preferred_element_type=F32).astype(BF16)
    z_ref[...] = jnp.dot(xn, wz_ref[...], preferred_element_type=F32)


def _gla_inproj(h, g, w_main, w_z):
    t = h.shape[0]
    tm = ROW_TILE
    row = lambda i: (i, 0)
    return pl.pallas_call(
        _gla_in_kernel,
        grid=(t // tm,),
        in_specs=[
            pl.BlockSpec((tm, D_MODEL), row),
            _const_spec((1, D_MODEL)),
            _const_spec(w_main.shape),
            _const_spec(w_z.shape),
        ],
        out_specs=[
            pl.BlockSpec((tm, GLA_QK), row),
            pl.BlockSpec((tm, GLA_QK), row),
            pl.BlockSpec((tm, GLA_V), row),
            pl.BlockSpec((tm, GLA_V), row),
            pl.BlockSpec((tm, LANES), row),
        ],
        out_shape=[
            jax.ShapeDtypeStruct((t, GLA_QK), BF16),
            jax.ShapeDtypeStruct((t, GLA_QK), BF16),
            jax.ShapeDtypeStruct((t, GLA_V), BF16),
            jax.ShapeDtypeStruct((t, GLA_V), BF16),
            jax.ShapeDtypeStruct((t, LANES), F32),
        ],
        compiler_params=pltpu.CompilerParams(
            dimension_semantics=("arbitrary",), vmem_limit_bytes=VMEM_LIMIT_BYTES),
        name="gla_inproj",
    )(h, g, w_main, w_z)


def _gla_kernel(q_ref, k_ref, v_ref, r_ref, z_ref, wcat_ref, bias_ref, og_ref, cum_ref, out_ref,
                lhs_scr, kv_scr, dec_scr):
    seq = q_ref.shape[0]
    c64 = GLA_CHUNK
    tile = GLA_TILE
    n_tiles = seq // tile
    grp = GLA_GROUP
    grp_rows = grp * tile
    dk = GLA_DK

    ridx = lax.broadcasted_iota(jnp.int32, (tile, tile), 0)
    cidx = lax.broadcasted_iota(jnp.int32, (tile, tile), 1)
    same_chunk = (ridx // c64) == (cidx // c64)
    fwd_diag = same_chunk & (ridx >= cidx)
    fwd_cross = (ridx >= c64) & (cidx < c64)
    bwd_diag = same_chunk & (ridx <= cidx)
    bwd_cross = (ridx < c64) & (cidx >= c64)
    second = lax.broadcasted_iota(jnp.int32, (tile, dk), 0) >= c64
    zeros_td = jnp.zeros((tile, dk), BF16)
    zeros_2td = jnp.zeros((2 * tile, dk), BF16)

    def phase_a(g, carry):
        goff = pl.multiple_of(g * grp_rows, grp_rows)
        z_hi, z_lo = _hi_lo(z_ref[pl.ds(goff, grp_rows), :])
        zcat = jnp.concatenate([z_hi, z_lo, z_hi], axis=1)
        logit = jnp.dot(zcat, wcat_ref[...], preferred_element_type=F32) + bias_ref[...]
        lg2 = (jnp.minimum(logit, 0.0) - jnp.log1p(jnp.exp(-jnp.abs(logit)))) * (1.0 / GATE_TAU)

        sums = []
        for t in range(grp):
            lg_hi, lg_lo = _hi_lo(lg2[t * tile:(t + 1) * tile])
            rhs = jnp.concatenate([
                jnp.concatenate([lg_hi[:, :dk], zeros_td], axis=1),
                jnp.concatenate([lg_lo[:, :dk], zeros_td], axis=1),
                jnp.concatenate([zeros_td, lg_hi[:, dk:]], axis=1),
                jnp.concatenate([zeros_td, lg_lo[:, dk:]], axis=1)], axis=0)
            sums.append(jnp.dot(cum_ref[...], rhs, preferred_element_type=F32))

        scores = []
        kvs = []
        for t in range(grp):
            rows = pl.ds(goff + t * tile, tile)
            qf = q_ref[rows, :].astype(F32)
            kf = k_ref[rows, :].astype(F32)
            vv = v_ref[rows, :]
            b_f = sums[t][:, :dk]
            b_b = sums[t][:, dk:]
            tf0, tf1 = b_f[c64 - 1:c64], b_f[tile - 1:tile]
            tb0, tb1 = b_b[0:1], b_b[c64:c64 + 1]
            qe_f = (qf * jnp.exp(b_f)).astype(BF16)
            ke_f = (kf * jnp.exp(-b_f)).astype(BF16)
            kd_f32 = kf * jnp.exp(jnp.where(second, tf1, tf0) - b_f)
            qe_b = (qf * jnp.exp(b_b)).astype(BF16)
            ke_b = (kf * jnp.exp(-b_b)).astype(BF16)
            kd_b32 = kf * jnp.exp(jnp.where(second, tb1, tb0) - b_b)
            qt_f = (qf * jnp.exp(b_f) * jnp.where(second, jnp.exp(tf0), 1.0)).astype(BF16)
            qt_b = (qf * jnp.exp(b_b) * jnp.where(second, 1.0, jnp.exp(tb1))).astype(BF16)
            kt_f = kd_f32 * jnp.where(second, 1.0, jnp.exp(tf1))
            kt_b = kd_b32 * jnp.where(second, jnp.exp(tb0), 1.0)

            keys = jnp.concatenate([
                jnp.concatenate([ke_f, kd_f32.astype(BF16)], axis=0),
                jnp.concatenate([ke_b, kd_b32.astype(BF16)], axis=0)], axis=0)
            keys = jnp.concatenate([
                jnp.concatenate([keys[:2 * tile], zeros_2td], axis=1),
                jnp.concatenate([zeros_2td, keys[2 * tile:]], axis=1)], axis=0)
            sc = lax.dot_general(jnp.concatenate([qe_f, qe_b], axis=1), keys,
                                 (((1,), (1,)), ((), ())), preferred_element_type=F32)
            scores.append((sc, qt_f, qt_b))

            kt_both = jnp.concatenate([
                jnp.concatenate([kt_f.T.astype(BF16), zeros_td], axis=1),
                jnp.concatenate([zeros_td, kt_b.T.astype(BF16)], axis=1)], axis=0)
            kvs.append(jnp.dot(kt_both, jnp.concatenate([vv, vv], axis=0),
                               preferred_element_type=F32))
            idx = g * grp + t
            dec_scr[0, idx] = jnp.broadcast_to(jnp.exp(tf0 + tf1), (dk, dk)).T
            dec_scr[1, idx] = jnp.broadcast_to(jnp.exp(tb0 + tb1), (dk, dk)).T

        for t in range(grp):
            rows = pl.ds(goff + t * tile, tile)
            sc, qt_f, qt_b = scores[t]
            a_sum = (jnp.where(fwd_diag, sc[:, :tile], 0.0)
                     + jnp.where(fwd_cross, sc[:, tile:2 * tile], 0.0)
                     + jnp.where(bwd_diag, sc[:, 2 * tile:3 * tile], 0.0)
                     + jnp.where(bwd_cross, sc[:, 3 * tile:], 0.0))
            lhs_scr[rows, :] = jnp.concatenate([a_sum.astype(BF16), qt_f, qt_b], axis=1)
            idx = g * grp + t
            kv_scr[0, idx] = kvs[t][:dk]
            kv_scr[1, idx] = kvs[t][dk:]
        return carry

    lax.fori_loop(0, n_tiles // grp, phase_a, 0)

    for d in range(2):
        def step(i, run, d=d):
            t = i if d == 0 else n_tiles - 1 - i
            inc = kv_scr[d, t]
            kv_scr[d, t] = run
            dec = dec_scr[d, t]
            return jnp.concatenate([dec, dec], axis=1) * run + inc

        lax.fori_loop(0, n_tiles, step, jnp.zeros((dk, GLA_DV), F32))

    def phase_c(g, carry):
        goff = pl.multiple_of(g * grp_rows, grp_rows)
        outs = []
        for t in range(grp):
            rows = pl.ds(goff + t * tile, tile)
            idx = g * grp + t
            rhs = jnp.concatenate([v_ref[rows, :], kv_scr[0, idx].astype(BF16),
                                   kv_scr[1, idx].astype(BF16)], axis=0)
            outs.append(jnp.dot(lhs_scr[rows, :], rhs, preferred_element_type=F32))
        for t in range(grp):
            rows = pl.ds(goff + t * tile, tile)
            on = _rms(outs[t], og_ref[...])
            rr = r_ref[rows, :].astype(F32)
            out_ref[rows, :] = (on * (rr * jax.nn.sigmoid(rr))).astype(BF16)
        return carry

    lax.fori_loop(0, n_tiles // grp, phase_c, 0)


def _gla_scan(q, k, v, r, z, wcat, bias, og, cum, batch, seq):
    t = q.shape[0]
    n_tiles = seq // GLA_TILE
    bh = lambda b, h: (b, h)
    return pl.pallas_call(
        _gla_kernel,
        grid=(batch, GLA_HEADS),
        in_specs=[
            pl.BlockSpec((seq, GLA_DK), bh),
            pl.BlockSpec((seq, GLA_DK), bh),
            pl.BlockSpec((seq, GLA_DV), bh),
            pl.BlockSpec((seq, GLA_DV), bh),
            pl.BlockSpec((seq, LANES), lambda b, h: (b, 0)),
            pl.BlockSpec((3 * LANES, 2 * GLA_DK), lambda b, h: (0, h)),
            pl.BlockSpec((1, 2 * GLA_DK), lambda b, h: (0, h)),
            _const_spec((1, GLA_DV)),
            _const_spec(cum.shape),
        ],
        out_specs=pl.BlockSpec((seq, GLA_DV), bh),
        out_shape=jax.ShapeDtypeStruct((t, GLA_V), BF16),
        scratch_shapes=[
            pltpu.VMEM((seq, GLA_TILE + 2 * GLA_DK), BF16),
            pltpu.VMEM((2, n_tiles, GLA_DK, GLA_DV), F32),
            pltpu.VMEM((2, n_tiles, GLA_DK, LANES), F32),
        ],
        compiler_params=pltpu.CompilerParams(
            dimension_semantics=("arbitrary", "arbitrary"),
            vmem_limit_bytes=VMEM_LIMIT_BYTES),
        name="gla_scan",
    )(q, k, v, r, z, wcat, bias, og, cum)


def _rope_tables(seq):
    pos = jnp.arange(seq, dtype=jnp.int32)
    half = HEAD_DIM // 2
    inv = ROPE_THETA ** (-jnp.arange(0, half, 2, dtype=F32) / half)
    ang_r = (pos // GRID_W).astype(F32)[:, None] * inv[None, :]
    ang_c = (pos % GRID_W).astype(F32)[:, None] * inv[None, :]
    cos_h = jnp.concatenate([jnp.cos(ang_r)] * 2 + [jnp.cos(ang_c)] * 2, axis=-1)
    sin_h = jnp.concatenate([-jnp.sin(ang_r), jnp.sin(ang_r), -jnp.sin(ang_c), jnp.sin(ang_c)], axis=-1)
    return jnp.tile(cos_h, (1, 2)), jnp.tile(sin_h, (1, 2))


def _gla_cumsum_matrix():
    n = GLA_TILE
    r = jnp.arange(n)[:, None]
    c = jnp.arange(n)[None, :]
    same = (r // GLA_CHUNK) == (c // GLA_CHUNK)
    prefix = (same & (c <= r)).astype(BF16)
    suffix = (same & (c >= r)).astype(BF16)
    return jnp.concatenate([prefix, prefix, suffix, suffix], axis=1)


def kernel(x, norm_mix, norm_mlp, attn_w_qkv, attn_q_norm, attn_k_norm, attn_w_o,
           gla_w_in, gla_w_gate_up, gla_b_gate, gla_out_norm, gla_w_o,
           mlp_w_in, mlp_w_out, final_norm):
    batch, seq, d = x.shape
    t = batch * seq
    h = x.reshape(t, d)
    row = lambda a: a.reshape(1, -1).astype(F32)

    cos, sin = _rope_tables(seq)
    gidx = jnp.arange(LANES)
    gsum = (gidx[:, None] // HEAD_DIM == gidx[None, :] // HEAD_DIM).astype(BF16)
    q_gain = jnp.tile(attn_q_norm[0], 2).reshape(1, LANES) * (LOG2E / math.sqrt(HEAD_DIM))
    k_gain = jnp.tile(attn_k_norm[0], 2).reshape(1, LANES)
    q, kt, v = _qkv_proj(h, row(norm_mix[0]), attn_w_qkv[0].astype(BF16), q_gain, k_gain,
                         cos, sin, gsum, batch, seq)
    o = _attention(q, kt, v, batch, seq)
    h = _oproj_mlp(h, o, attn_w_o[0].astype(BF16), row(norm_mlp[0]),
                   mlp_w_in[0].astype(BF16), mlp_w_out[0].astype(BF16), row(final_norm), False)

    n_main = 2 * GLA_QK + 2 * GLA_V
    w_in = gla_w_in[0]
    w_main = w_in[:, :n_main].astype(BF16)
    w_z = jnp.pad(w_in[:, n_main:], ((0, 0), (0, LANES - 2 * GATE_RANK))).astype(BF16)
    gq, gk, gv, gr, gz = _gla_inproj(h, row(norm_mix[1]), w_main, w_z)
    wup = jnp.zeros((2, LANES, GLA_QK), F32)
    wup = wup.at[0, :GATE_RANK].set(gla_w_gate_up[0, 0])
    wup = wup.at[1, GATE_RANK:2 * GATE_RANK].set(gla_w_gate_up[0, 1])
    wup = wup.reshape(2, LANES, GLA_HEADS, GLA_DK).transpose(1, 2, 0, 3).reshape(LANES, 2 * GLA_QK)
    wup_hi, wup_lo = _hi_lo(wup)
    wcat = jnp.concatenate([wup_hi, wup_hi, wup_lo], axis=0)
    b_gate = gla_b_gate[0].reshape(2, GLA_HEADS, GLA_DK).transpose(1, 0, 2).reshape(1, 2 * GLA_QK)
    go = _gla_scan(gq, gk, gv, gr, gz, wcat, b_gate,
                   row(gla_out_norm[0]), _gla_cumsum_matrix(), batch, seq)
    h = _oproj_mlp(h, go, gla_w_o[0].astype(BF16), row(norm_mlp[1]),
                   mlp_w_in[1].astype(BF16), mlp_w_out[1].astype(BF16), row(final_norm), True)
    return h.reshape(batch, seq, d)
```

```python
import functools
import math

import jax
import jax.numpy as jnp
from jax import lax
from jax.experimental import pallas as pl
from jax.experimental.pallas import tpu as pltpu

F32 = jnp.float32
BF16 = jnp.bfloat16

D_MODEL = 1024
GRID_W = 64
N_HEADS = 16
N_KV_HEADS = 4
HEAD_DIM = 64
GROUP = N_HEADS // N_KV_HEADS
ROPE_THETA = 10000.0
GLA_HEADS = 4
GLA_DK = 128
GLA_DV = 256
GLA_QK = GLA_HEADS * GLA_DK
GLA_V = GLA_HEADS * GLA_DV
GATE_RANK = 16
GATE_TAU = 16.0
D_FF = 4 * D_MODEL
EPS = 1e-6

LANES = 128
VMEM_LIMIT_BYTES = 56 * 1024 * 1024

ROW_TILE = 512
ATTN_TQ = 512
ATTN_TK = 512
FF_TILE = 1024
GLA_CHUNK = 64
GLA_TILE = 2 * GLA_CHUNK
GLA_GROUP = 4
LOG2E = 1.4426950408889634


def _const_spec(shape):
    nd = len(shape)
    return pl.BlockSpec(shape, lambda *_: (0,) * nd, pipeline_mode=pl.Buffered(1))


def _rms(x, gain):
    ms = jnp.mean(x * x, axis=-1, keepdims=True)
    return x * lax.rsqrt(ms + EPS) * gain


def _hi_lo(x_f32):
    hi = x_f32.astype(BF16)
    lo = (x_f32 - hi.astype(F32)).astype(BF16)
    return hi, lo


def _qkv_kernel(x_ref, g_ref, w_ref, qg_ref, kg_ref, cos_ref, sin_ref, gsum_ref,
                q_ref, kt_ref, v_ref):
    tm = x_ref.shape[0]
    xn = _rms(x_ref[...], g_ref[...]).astype(BF16)
    cos = cos_ref[...]
    sin = sin_ref[...]
    lane = lax.broadcasted_iota(jnp.int32, (tm, LANES), 1)
    first_half = (lane % 32) < 16
    gsum = gsum_ref[...]

    def norm_rope(c, gain):
        ss = jnp.dot((c * c).astype(BF16), gsum, preferred_element_type=F32)
        cn = c * lax.rsqrt(ss * (1.0 / HEAD_DIM) + EPS) * gain
        rot = jnp.where(first_half, pltpu.roll(cn, LANES - 16, 1), pltpu.roll(cn, 16, 1))
        return cn * cos + rot * sin

    y = jnp.dot(xn, w_ref[...], preferred_element_type=F32)
    nq = N_HEADS * HEAD_DIM
    nk = N_KV_HEADS * HEAD_DIM
    for j in range(nq // LANES):
        sl = slice(j * LANES, (j + 1) * LANES)
        q_ref[:, sl] = norm_rope(y[:, sl], qg_ref[...]).astype(BF16)
    for j in range(nk // LANES):
        kt = norm_rope(y[:, nq + j * LANES: nq + (j + 1) * LANES], kg_ref[...]).T
        kt_ref[0, 2 * j] = kt[:HEAD_DIM].astype(BF16)
        kt_ref[0, 2 * j + 1] = kt[HEAD_DIM:].astype(BF16)
    for g in range(N_KV_HEADS):
        c0 = nq + nk + (g // 2) * LANES
        pair = y[:, c0:c0 + LANES]
        if g % 2:
            pair = pltpu.roll(pair, HEAD_DIM, 1)
        v_ref[:, g * LANES:(g + 1) * LANES] = jnp.where(lane < HEAD_DIM, pair, 1.0).astype(BF16)


def _qkv_proj(x2, g, w, qg, kg, cos, sin, gsum, batch, seq):
    t = x2.shape[0]
    tm = ROW_TILE
    spb = seq // tm
    nkv = N_KV_HEADS * LANES
    return pl.pallas_call(
        _qkv_kernel,
        grid=(t // tm,),
        in_specs=[
            pl.BlockSpec((tm, D_MODEL), lambda i: (i, 0)),
            _const_spec((1, D_MODEL)),
            _const_spec(w.shape),
            _const_spec((1, LANES)),
            _const_spec((1, LANES)),
            pl.BlockSpec((tm, LANES), lambda i: (i % spb, 0)),
            pl.BlockSpec((tm, LANES), lambda i: (i % spb, 0)),
            _const_spec((LANES, LANES)),
        ],
        out_specs=[
            pl.BlockSpec((tm, N_HEADS * HEAD_DIM), lambda i: (i, 0)),
            pl.BlockSpec((1, N_KV_HEADS, HEAD_DIM, tm), lambda i: (i // spb, 0, 0, i % spb)),
            pl.BlockSpec((tm, nkv), lambda i: (i, 0)),
        ],
        out_shape=[
            jax.ShapeDtypeStruct((t, N_HEADS * HEAD_DIM), BF16),
            jax.ShapeDtypeStruct((batch, N_KV_HEADS, HEAD_DIM, seq), BF16),
            jax.ShapeDtypeStruct((t, nkv), BF16),
        ],
        compiler_params=pltpu.CompilerParams(
            dimension_semantics=("arbitrary",), vmem_limit_bytes=VMEM_LIMIT_BYTES),
        name="qkv_proj",
    )(x2, g, w, qg, kg, cos, sin, gsum)


def _attn_kernel(q_ref, kt_ref, v_ref, o_ref):
    tq = q_ref.shape[0]
    seq = v_ref.shape[0]
    q4 = jnp.concatenate(
        [q_ref[:, h * HEAD_DIM:(h + 1) * HEAD_DIM] for h in range(GROUP)], axis=0)
    m = None
    acc = None
    for c in range(seq // ATTN_TK):
        ks = slice(c * ATTN_TK, (c + 1) * ATTN_TK)
        s = jnp.dot(q4, kt_ref[0, 0, :, ks], preferred_element_type=F32)
        m_c = jnp.max(s, axis=-1, keepdims=True)
        m_new = m_c if m is None else jnp.maximum(m, m_c)
        p = jnp.exp2((s - m_new).astype(BF16))
        pv = jnp.dot(p, v_ref[ks, :], preferred_element_type=F32)
        acc = pv if acc is None else acc * jnp.exp2(m - m_new) + pv
        m = m_new
    out = acc * (1.0 / pltpu.roll(acc, HEAD_DIM, 1))
    lane = lax.broadcasted_iota(jnp.int32, (tq, LANES), 1)
    for hp in range(GROUP // 2):
        even = out[(2 * hp) * tq:(2 * hp + 1) * tq]
        odd = pltpu.roll(out[(2 * hp + 1) * tq:(2 * hp + 2) * tq], HEAD_DIM, 1)
        o_ref[:, hp * LANES:(hp + 1) * LANES] = jnp.where(lane < HEAD_DIM, even, odd).astype(BF16)


def _attention(q, kt, v, batch, seq):
    t = q.shape[0]
    tq = ATTN_TQ
    nq = seq // tq
    grp_w = GROUP * HEAD_DIM
    return pl.pallas_call(
        _attn_kernel,
        grid=(batch, N_KV_HEADS, nq),
        in_specs=[
            pl.BlockSpec((tq, grp_w), lambda b, g, i: (b * nq + i, g)),
            pl.BlockSpec((1, 1, HEAD_DIM, seq), lambda b, g, i: (b, g, 0, 0)),
            pl.BlockSpec((seq, LANES), lambda b, g, i: (b, g)),
        ],
        out_specs=pl.BlockSpec((tq, grp_w), lambda b, g, i: (b * nq + i, g)),
        out_shape=jax.ShapeDtypeStruct((t, N_HEADS * HEAD_DIM), BF16),
        compiler_params=pltpu.CompilerParams(
            dimension_semantics=("arbitrary", "arbitrary", "arbitrary"),
            vmem_limit_bytes=VMEM_LIMIT_BYTES),
        name="attention",
    )(q, kt, v)


def _oproj_mlp_kernel(h_ref, o_ref, wo_ref, g_ref, w1_ref, w2_ref, gf_ref, out_ref, *, final):
    h1 = h_ref[...] + jnp.dot(o_ref[...], wo_ref[...], preferred_element_type=F32)
    hn = _rms(h1, g_ref[...]).astype(BF16)
    acc = h1
    for c in range(D_FF // FF_TILE):
        sl = slice(c * FF_TILE, (c + 1) * FF_TILE)
        u = jnp.dot(hn, w1_ref[:, sl], preferred_element_type=F32)
        a = jnp.square(jnp.maximum(u, 0.0)).astype(BF16)
        acc = acc + jnp.dot(a, w2_ref[sl, :], preferred_element_type=F32)
    if final:
        acc = _rms(acc, gf_ref[...])
    out_ref[...] = acc


def _oproj_mlp(h, o, wo, g, w1_all, w2_all, layer, gf, final):
    t = h.shape[0]
    tm = ROW_TILE
    layer_spec = lambda a: pl.BlockSpec((None,) + a.shape[1:], lambda i: (layer, 0, 0),
                                        pipeline_mode=pl.Buffered(1))
    return pl.pallas_call(
        functools.partial(_oproj_mlp_kernel, final=final),
        grid=(t // tm,),
        in_specs=[
            pl.BlockSpec((tm, D_MODEL), lambda i: (i, 0)),
            pl.BlockSpec((tm, o.shape[1]), lambda i: (i, 0)),
            _const_spec(wo.shape),
            _const_spec((1, D_MODEL)),
            layer_spec(w1_all),
            layer_spec(w2_all),
            _const_spec((1, D_MODEL)),
        ],
        out_specs=pl.BlockSpec((tm, D_MODEL), lambda i: (i, 0)),
        out_shape=jax.ShapeDtypeStruct((t, D_MODEL), F32),
        compiler_params=pltpu.CompilerParams(
            dimension_semantics=("arbitrary",), vmem_limit_bytes=VMEM_LIMIT_BYTES),
        name="oproj_mlp_final" if final else "oproj_mlp",
    )(h, o, wo, g, w1_all, w2_all, gf)


def _gla_in_kernel(x_ref, g_ref, w_ref, q_ref, k_ref, v_ref, r_ref, z_ref):
    tm = x_ref.shape[0]
    xn = _rms(x_ref[...], g_ref[...]).astype(BF16)
    qk = jnp.dot(xn, w_ref[:, :2 * GLA_QK], preferred_element_type=F32)
    q_ref[...] = (qk[:, :GLA_QK] * (GLA_DK ** -0.5)).astype(BF16)
    k_ref[...] = qk[:, GLA_QK:].astype(BF16)
    v0 = 2 * GLA_QK
    r0 = v0 + GLA_V
    z0 = r0 + GLA_V
    v_ref[...] = jnp.dot(xn, w_ref[:, v0:r0], preferred_element_type=F32).astype(BF16)
    r_ref[...] = jnp.dot(xn, w_ref[:, r0:z0], preferred_element_type=F32).astype(BF16)
    codes = jnp.dot(xn, w_ref[:, z0:], preferred_element_type=F32)
    z_ref[...] = jnp.concatenate(
        [codes, jnp.zeros((tm, LANES - 2 * GATE_RANK), F32)], axis=1)


def _gla_inproj(h, g, w_in):
    t = h.shape[0]
    tm = ROW_TILE
    row = lambda i: (i, 0)
    return pl.pallas_call(
        _gla_in_kernel,
        grid=(t // tm,),
        in_specs=[
            pl.BlockSpec((tm, D_MODEL), row),
            _const_spec((1, D_MODEL)),
            _const_spec(w_in.shape),
        ],
        out_specs=[
            pl.BlockSpec((tm, GLA_QK), row),
            pl.BlockSpec((tm, GLA_QK), row),
            pl.BlockSpec((tm, GLA_V), row),
            pl.BlockSpec((tm, GLA_V), row),
            pl.BlockSpec((tm, LANES), row),
        ],
        out_shape=[
            jax.ShapeDtypeStruct((t, GLA_QK), BF16),
            jax.ShapeDtypeStruct((t, GLA_QK), BF16),
            jax.ShapeDtypeStruct((t, GLA_V), BF16),
            jax.ShapeDtypeStruct((t, GLA_V), BF16),
            jax.ShapeDtypeStruct((t, LANES), F32),
        ],
        compiler_params=pltpu.CompilerParams(
            dimension_semantics=("arbitrary",), vmem_limit_bytes=VMEM_LIMIT_BYTES),
        name="gla_inproj",
    )(h, g, w_in)


def _gla_kernel(q_ref, k_ref, v_ref, r_ref, z_ref, wcat_ref, bias_ref, og_ref, cum_ref, out_ref,
                lhs_scr, kv_scr, dec_scr):
    seq = q_ref.shape[0]
    c64 = GLA_CHUNK
    tile = GLA_TILE
    n_tiles = seq // tile
    grp = GLA_GROUP
    grp_rows = grp * tile
    dk = GLA_DK

    ridx = lax.broadcasted_iota(jnp.int32, (tile, tile), 0)
    cidx = lax.broadcasted_iota(jnp.int32, (tile, tile), 1)
    same_chunk = (ridx // c64) == (cidx // c64)
    fwd_diag = same_chunk & (ridx >= cidx)
    fwd_cross = (ridx >= c64) & (cidx < c64)
    bwd_diag = same_chunk & (ridx <= cidx)
    bwd_cross = (ridx < c64) & (cidx >= c64)
    second = lax.broadcasted_iota(jnp.int32, (tile, dk), 0) >= c64
    zeros_td = jnp.zeros((tile, dk), BF16)
    zeros_2td = jnp.zeros((2 * tile, dk), BF16)

    def phase_a(g, carry):
        goff = pl.multiple_of(g * grp_rows, grp_rows)
        z_hi, z_lo = _hi_lo(z_ref[pl.ds(goff, grp_rows), :])
        zcat = jnp.concatenate([z_hi, z_lo, z_hi], axis=1)
        logit = jnp.dot(zcat, wcat_ref[...], preferred_element_type=F32) + bias_ref[...]
        lg2 = (jnp.minimum(logit, 0.0) - jnp.log(1.0 + jnp.exp(-jnp.abs(logit)))) * (1.0 / GATE_TAU)

        sums = []
        for t in range(grp):
            lg_hi, lg_lo = _hi_lo(lg2[t * tile:(t + 1) * tile])
            rhs = jnp.concatenate([
                jnp.concatenate([lg_hi[:, :dk], zeros_td], axis=1),
                jnp.concatenate([lg_lo[:, :dk], zeros_td], axis=1),
                jnp.concatenate([zeros_td, lg_hi[:, dk:]], axis=1),
                jnp.concatenate([zeros_td, lg_lo[:, dk:]], axis=1)], axis=0)
            sums.append(jnp.dot(cum_ref[...], rhs, preferred_element_type=F32))

        scores = []
        kvs = []
        for t in range(grp):
            rows = pl.ds(goff + t * tile, tile)
            qf = q_ref[rows, :].astype(F32)
            kf = k_ref[rows, :].astype(F32)
            vv = v_ref[rows, :]
            b_f = sums[t][:, :dk]
            b_b = sums[t][:, dk:]
            tf0, tf1 = b_f[c64 - 1:c64], b_f[tile - 1:tile]
            tb0, tb1 = b_b[0:1], b_b[c64:c64 + 1]
            qe_f = (qf * jnp.exp(b_f)).astype(BF16)
            ke_f = (kf * jnp.exp(-b_f)).astype(BF16)
            kd_f32 = kf * jnp.exp(jnp.where(second, tf1, tf0) - b_f)
            qe_b = (qf * jnp.exp(b_b)).astype(BF16)
            ke_b = (kf * jnp.exp(-b_b)).astype(BF16)
            kd_b32 = kf * jnp.exp(jnp.where(second, tb1, tb0) - b_b)
            qt_f = (qf * jnp.exp(b_f) * jnp.where(second, jnp.exp(tf0), 1.0)).astype(BF16)
            qt_b = (qf * jnp.exp(b_b) * jnp.where(second, 1.0, jnp.exp(tb1))).astype(BF16)
            kt_f = kd_f32 * jnp.where(second, 1.0, jnp.exp(tf1))
            kt_b = kd_b32 * jnp.where(second, jnp.exp(tb0), 1.0)

            keys = jnp.concatenate([
                jnp.concatenate([ke_f, kd_f32.astype(BF16)], axis=0),
                jnp.concatenate([ke_b, kd_b32.astype(BF16)], axis=0)], axis=0)
            keys = jnp.concatenate([
                jnp.concatenate([keys[:2 * tile], zeros_2td], axis=1),
                jnp.concatenate([zeros_2td, keys[2 * tile:]], axis=1)], axis=0)
            sc = lax.dot_general(jnp.concatenate([qe_f, qe_b], axis=1), keys,
                                 (((1,), (1,)), ((), ())), preferred_element_type=F32)
            scores.append((sc, qt_f, qt_b))

            kt_both = jnp.concatenate([
                jnp.concatenate([kt_f.T.astype(BF16), zeros_td], axis=1),
                jnp.concatenate([zeros_td, kt_b.T.astype(BF16)], axis=1)], axis=0)
            kvs.append(jnp.dot(kt_both, jnp.concatenate([vv, vv], axis=0),
                               preferred_element_type=F32))
            idx = g * grp + t
            dec_scr[0, idx] = jnp.broadcast_to(jnp.exp(tf0 + tf1), (dk, dk)).T
            dec_scr[1, idx] = jnp.broadcast_to(jnp.exp(tb0 + tb1), (dk, dk)).T

        for t in range(grp):
            rows = pl.ds(goff + t * tile, tile)
            sc, qt_f, qt_b = scores[t]
            a_sum = (jnp.where(fwd_diag, sc[:, :tile], 0.0)
                     + jnp.where(fwd_cross, sc[:, tile:2 * tile], 0.0)
                     + jnp.where(bwd_diag, sc[:, 2 * tile:3 * tile], 0.0)
                     + jnp.where(bwd_cross, sc[:, 3 * tile:], 0.0))
            lhs_scr[rows, :] = jnp.concatenate([a_sum.astype(BF16), qt_f, qt_b], axis=1)
            idx = g * grp + t
            kv_scr[0, idx] = kvs[t][:dk]
            kv_scr[1, idx] = kvs[t][dk:]
        return carry

    lax.fori_loop(0, n_tiles // grp, phase_a, 0)

    for d in range(2):
        def step(i, run, d=d):
            t = i if d == 0 else n_tiles - 1 - i
            inc = kv_scr[d, t]
            kv_scr[d, t] = run
            dec = dec_scr[d, t]
            return jnp.concatenate([dec, dec], axis=1) * run + inc

        lax.fori_loop(0, n_tiles, step, jnp.zeros((dk, GLA_DV), F32))

    def phase_c(g, carry):
        goff = pl.multiple_of(g * grp_rows, grp_rows)
        outs = []
        for t in range(grp):
            rows = pl.ds(goff + t * tile, tile)
            idx = g * grp + t
            rhs = jnp.concatenate([v_ref[rows, :], kv_scr[0, idx].astype(BF16),
                                   kv_scr[1, idx].astype(BF16)], axis=0)
            outs.append(jnp.dot(lhs_scr[rows, :], rhs, preferred_element_type=F32))
        for t in range(grp):
            rows = pl.ds(goff + t * tile, tile)
            on = _rms(outs[t], og_ref[...])
            rr = r_ref[rows, :].astype(F32)
            out_ref[rows, :] = (on * (rr * jax.nn.sigmoid(rr))).astype(BF16)
        return carry

    lax.fori_loop(0, n_tiles // grp, phase_c, 0)


def _gla_scan(q, k, v, r, z, wcat, bias, og, cum, batch, seq):
    t = q.shape[0]
    n_tiles = seq // GLA_TILE
    bh = lambda b, h: (b, h)
    return pl.pallas_call(
        _gla_kernel,
        grid=(batch, GLA_HEADS),
        in_specs=[
            pl.BlockSpec((seq, GLA_DK), bh),
            pl.BlockSpec((seq, GLA_DK), bh),
            pl.BlockSpec((seq, GLA_DV), bh),
            pl.BlockSpec((seq, GLA_DV), bh),
            pl.BlockSpec((seq, LANES), lambda b, h: (b, 0)),
            pl.BlockSpec((3 * LANES, 2 * GLA_DK), lambda b, h: (0, h)),
            pl.BlockSpec((1, 2 * GLA_DK), lambda b, h: (0, h)),
            _const_spec((1, GLA_DV)),
            _const_spec(cum.shape),
        ],
        out_specs=pl.BlockSpec((seq, GLA_DV), bh),
        out_shape=jax.ShapeDtypeStruct((t, GLA_V), BF16),
        scratch_shapes=[
            pltpu.VMEM((seq, GLA_TILE + 2 * GLA_DK), BF16),
            pltpu.VMEM((2, n_tiles, GLA_DK, GLA_DV), F32),
            pltpu.VMEM((2, n_tiles, GLA_DK, LANES), F32),
        ],
        compiler_params=pltpu.CompilerParams(
            dimension_semantics=("arbitrary", "arbitrary"),
            vmem_limit_bytes=VMEM_LIMIT_BYTES),
        name="gla_scan",
    )(q, k, v, r, z, wcat, bias, og, cum)


def _rope_tables(seq):
    pos = jnp.arange(seq, dtype=jnp.int32)
    half = HEAD_DIM // 2
    inv = ROPE_THETA ** (-jnp.arange(0, half, 2, dtype=F32) / half)
    ang_r = (pos // GRID_W).astype(F32)[:, None] * inv[None, :]
    ang_c = (pos % GRID_W).astype(F32)[:, None] * inv[None, :]
    cos_h = jnp.concatenate([jnp.cos(ang_r)] * 2 + [jnp.cos(ang_c)] * 2, axis=-1)
    sin_h = jnp.concatenate([-jnp.sin(ang_r), jnp.sin(ang_r), -jnp.sin(ang_c), jnp.sin(ang_c)], axis=-1)
    return jnp.tile(cos_h, (1, 2)), jnp.tile(sin_h, (1, 2))


def _gla_cumsum_matrix():
    n = GLA_TILE
    r = jnp.arange(n)[:, None]
    c = jnp.arange(n)[None, :]
    same = (r // GLA_CHUNK) == (c // GLA_CHUNK)
    prefix = (same & (c <= r)).astype(BF16)
    suffix = (same & (c >= r)).astype(BF16)
    return jnp.concatenate([prefix, prefix, suffix, suffix], axis=1)


def kernel(x, norm_mix, norm_mlp, attn_w_qkv, attn_q_norm, attn_k_norm, attn_w_o,
           gla_w_in, gla_w_gate_up, gla_b_gate, gla_out_norm, gla_w_o,
           mlp_w_in, mlp_w_out, final_norm):
    batch, seq, d = x.shape
    t = batch * seq
    h = x.reshape(t, d)
    row = lambda a: a.reshape(1, -1).astype(F32)

    cos, sin = _rope_tables(seq)
    gidx = jnp.arange(LANES)
    gsum = (gidx[:, None] // HEAD_DIM == gidx[None, :] // HEAD_DIM).astype(BF16)
    q_gain = jnp.tile(attn_q_norm[0], 2).reshape(1, LANES) * (LOG2E / math.sqrt(HEAD_DIM))
    k_gain = jnp.tile(attn_k_norm[0], 2).reshape(1, LANES)
    q, kt, v = _qkv_proj(h, row(norm_mix[0]), attn_w_qkv[0].astype(BF16), q_gain, k_gain,
                         cos, sin, gsum, batch, seq)
    o = _attention(q, kt, v, batch, seq)
    w1_all = mlp_w_in.astype(BF16)
    w2_all = mlp_w_out.astype(BF16)
    h = _oproj_mlp(h, o, attn_w_o[0].astype(BF16), row(norm_mlp[0]),
                   w1_all, w2_all, 0, row(final_norm), False)

    gq, gk, gv, gr, gz = _gla_inproj(h, row(norm_mix[1]), gla_w_in[0].astype(BF16))
    wup = jnp.zeros((2, LANES, GLA_QK), F32)
    wup = wup.at[0, :GATE_RANK].set(gla_w_gate_up[0, 0])
    wup = wup.at[1, GATE_RANK:2 * GATE_RANK].set(gla_w_gate_up[0, 1])
    wup = wup.reshape(2, LANES, GLA_HEADS, GLA_DK).transpose(1, 2, 0, 3).reshape(LANES, 2 * GLA_QK)
    wup_hi, wup_lo = _hi_lo(wup)
    wcat = jnp.concatenate([wup_hi, wup_hi, wup_lo], axis=0)
    b_gate = gla_b_gate[0].reshape(2, GLA_HEADS, GLA_DK).transpose(1, 0, 2).reshape(1, 2 * GLA_QK)
    go = _gla_scan(gq, gk, gv, gr, gz, wcat, b_gate,
                   row(gla_out_norm[0]), _gla_cumsum_matrix(), batch, seq)
    h = _oproj_mlp(h, go, gla_w_o[0].astype(BF16), row(norm_mlp[1]),
                   w1_all, w2_all, 1, row(final_norm), True)
    return h.reshape(batch, seq, d)
```

```python
import functools
import math

import jax
import jax.numpy as jnp
from jax import lax
from jax.experimental import pallas as pl
from jax.experimental.pallas import tpu as pltpu

F32 = jnp.float32
BF16 = jnp.bfloat16

D_MODEL = 1024
GRID_W = 64
N_HEADS = 16
N_KV_HEADS = 4
HEAD_DIM = 64
GROUP = N_HEADS // N_KV_HEADS
ROPE_THETA = 10000.0
GLA_HEADS = 4
GLA_DK = 128
GLA_DV = 256
GLA_QK = GLA_HEADS * GLA_DK
GLA_V = GLA_HEADS * GLA_DV
GATE_RANK = 16
GATE_TAU = 16.0
D_FF = 4 * D_MODEL
EPS = 1e-6

LANES = 128
VMEM_LIMIT_BYTES = 56 * 1024 * 1024

ROW_TILE = 512
QKV_SPLIT = 4
ATTN_TQ = 512
ATTN_TK = 512
FF_TILE = 1024
GLA_CHUNK = 64
GLA_TILE = 2 * GLA_CHUNK
GLA_GROUP = 4
LOG2E = 1.4426950408889634


def _const_spec(shape):
    nd = len(shape)
    return pl.BlockSpec(shape, lambda *_: (0,) * nd, pipeline_mode=pl.Buffered(1))


def _rms(x, gain):
    ms = jnp.mean(x * x, axis=-1, keepdims=True)
    return x * lax.rsqrt(ms + EPS) * gain


def _hi_lo(x_f32):
    hi = x_f32.astype(BF16)
    lo = (x_f32 - hi.astype(F32)).astype(BF16)
    return hi, lo


def _qkv_kernel(x_ref, g_ref, w_ref, qg_ref, kg_ref, cos_ref, sin_ref, gsum_ref,
                q_ref, kt_ref, v_ref):
    tm = x_ref.shape[0]
    sub = tm // QKV_SPLIT
    lane = lax.broadcasted_iota(jnp.int32, (sub, LANES), 1)
    first_half = (lane % 32) < 16
    gsum = gsum_ref[...]
    nq = N_HEADS * HEAD_DIM
    nk = N_KV_HEADS * HEAD_DIM

    for r in range(QKV_SPLIT):
        rows = slice(r * sub, (r + 1) * sub)
        xn = _rms(x_ref[rows, :], g_ref[...]).astype(BF16)
        cos = cos_ref[rows, :]
        sin = sin_ref[rows, :]

        def norm_rope(c, gain):
            ss = jnp.dot((c * c).astype(BF16), gsum, preferred_element_type=F32)
            cn = c * lax.rsqrt(ss * (1.0 / HEAD_DIM) + EPS) * gain
            rot = jnp.where(first_half, pltpu.roll(cn, LANES - 16, 1), pltpu.roll(cn, 16, 1))
            return cn * cos + rot * sin

        y = jnp.dot(xn, w_ref[...], preferred_element_type=F32)
        for j in range(nq // LANES):
            sl = slice(j * LANES, (j + 1) * LANES)
            q_ref[rows, sl] = norm_rope(y[:, sl], qg_ref[...]).astype(BF16)
        for j in range(nk // LANES):
            kt = norm_rope(y[:, nq + j * LANES: nq + (j + 1) * LANES], kg_ref[...]).T
            kt_ref[0, 2 * j, :, rows] = kt[:HEAD_DIM].astype(BF16)
            kt_ref[0, 2 * j + 1, :, rows] = kt[HEAD_DIM:].astype(BF16)
        for g in range(N_KV_HEADS):
            c0 = nq + nk + (g // 2) * LANES
            pair = y[:, c0:c0 + LANES]
            if g % 2:
                pair = pltpu.roll(pair, HEAD_DIM, 1)
            v_ref[rows, g * LANES:(g + 1) * LANES] = jnp.where(lane < HEAD_DIM, pair, 1.0).astype(BF16)


def _qkv_proj(x2, g, w, qg, kg, cos, sin, gsum, batch, seq):
    t = x2.shape[0]
    tm = ROW_TILE
    spb = seq // tm
    nkv = N_KV_HEADS * LANES
    return pl.pallas_call(
        _qkv_kernel,
        grid=(t // tm,),
        in_specs=[
            pl.BlockSpec((tm, D_MODEL), lambda i: (i, 0)),
            _const_spec((1, D_MODEL)),
            _const_spec(w.shape),
            _const_spec((1, LANES)),
            _const_spec((1, LANES)),
            pl.BlockSpec((tm, LANES), lambda i: (i % spb, 0)),
            pl.BlockSpec((tm, LANES), lambda i: (i % spb, 0)),
            _const_spec((LANES, LANES)),
        ],
        out_specs=[
            pl.BlockSpec((tm, N_HEADS * HEAD_DIM), lambda i: (i, 0)),
            pl.BlockSpec((1, N_KV_HEADS, HEAD_DIM, tm), lambda i: (i // spb, 0, 0, i % spb)),
            pl.BlockSpec((tm, nkv), lambda i: (i, 0)),
        ],
        out_shape=[
            jax.ShapeDtypeStruct((t, N_HEADS * HEAD_DIM), BF16),
            jax.ShapeDtypeStruct((batch, N_KV_HEADS, HEAD_DIM, seq), BF16),
            jax.ShapeDtypeStruct((t, nkv), BF16),
        ],
        compiler_params=pltpu.CompilerParams(
            dimension_semantics=("arbitrary",), vmem_limit_bytes=VMEM_LIMIT_BYTES),
        name="qkv_proj",
    )(x2, g, w, qg, kg, cos, sin, gsum)


def _attn_kernel(q_ref, kt_ref, v_ref, o_ref):
    tq = q_ref.shape[0]
    seq = v_ref.shape[0]
    q4 = jnp.concatenate(
        [q_ref[:, h * HEAD_DIM:(h + 1) * HEAD_DIM] for h in range(GROUP)], axis=0)
    m = None
    acc = None
    for c in range(seq // ATTN_TK):
        ks = slice(c * ATTN_TK, (c + 1) * ATTN_TK)
        s = jnp.dot(q4, kt_ref[0, 0, :, ks], preferred_element_type=F32)
        m_c = jnp.max(s, axis=-1, keepdims=True)
        m_new = m_c if m is None else jnp.maximum(m, m_c)
        p = jnp.exp2((s - m_new).astype(BF16))
        pv = jnp.dot(p, v_ref[ks, :], preferred_element_type=F32)
        acc = pv if acc is None else acc * jnp.exp2(m - m_new) + pv
        m = m_new
    out = acc * (1.0 / pltpu.roll(acc, HEAD_DIM, 1))
    lane = lax.broadcasted_iota(jnp.int32, (tq, LANES), 1)
    for hp in range(GROUP // 2):
        even = out[(2 * hp) * tq:(2 * hp + 1) * tq]
        odd = pltpu.roll(out[(2 * hp + 1) * tq:(2 * hp + 2) * tq], HEAD_DIM, 1)
        o_ref[:, hp * LANES:(hp + 1) * LANES] = jnp.where(lane < HEAD_DIM, even, odd).astype(BF16)


def _attention(q, kt, v, batch, seq):
    t = q.shape[0]
    tq = ATTN_TQ
    nq = seq // tq
    grp_w = GROUP * HEAD_DIM
    return pl.pallas_call(
        _attn_kernel,
        grid=(batch, N_KV_HEADS, nq),
        in_specs=[
            pl.BlockSpec((tq, grp_w), lambda b, g, i: (b * nq + i, g)),
            pl.BlockSpec((1, 1, HEAD_DIM, seq), lambda b, g, i: (b, g, 0, 0)),
            pl.BlockSpec((seq, LANES), lambda b, g, i: (b, g)),
        ],
        out_specs=pl.BlockSpec((tq, grp_w), lambda b, g, i: (b * nq + i, g)),
        out_shape=jax.ShapeDtypeStruct((t, N_HEADS * HEAD_DIM), BF16),
        compiler_params=pltpu.CompilerParams(
            dimension_semantics=("arbitrary", "arbitrary", "arbitrary"),
            vmem_limit_bytes=VMEM_LIMIT_BYTES),
        name="attention",
    )(q, kt, v)


def _oproj_mlp_kernel(h_ref, o_ref, wo_ref, g_ref, w1_ref, w2_ref, gf_ref, out_ref, *, final):
    h1 = h_ref[...] + jnp.dot(o_ref[...], wo_ref[...], preferred_element_type=F32)
    hn = _rms(h1, g_ref[...]).astype(BF16)
    acc = h1
    for c in range(D_FF // FF_TILE):
        sl = slice(c * FF_TILE, (c + 1) * FF_TILE)
        u = jnp.dot(hn, w1_ref[:, sl], preferred_element_type=F32)
        a = jnp.square(jnp.maximum(u, 0.0)).astype(BF16)
        acc = acc + jnp.dot(a, w2_ref[sl, :], preferred_element_type=F32)
    if final:
        acc = _rms(acc, gf_ref[...])
    out_ref[...] = acc


def _oproj_mlp(h, o, wo, g, w1_all, w2_all, layer, gf, final):
    t = h.shape[0]
    tm = ROW_TILE
    layer_spec = lambda a: pl.BlockSpec((None,) + a.shape[1:], lambda i: (layer, 0, 0),
                                        pipeline_mode=pl.Buffered(1))
    return pl.pallas_call(
        functools.partial(_oproj_mlp_kernel, final=final),
        grid=(t // tm,),
        in_specs=[
            pl.BlockSpec((tm, D_MODEL), lambda i: (i, 0)),
            pl.BlockSpec((tm, o.shape[1]), lambda i: (i, 0)),
            _const_spec(wo.shape),
            _const_spec((1, D_MODEL)),
            layer_spec(w1_all),
            layer_spec(w2_all),
            _const_spec((1, D_MODEL)),
        ],
        out_specs=pl.BlockSpec((tm, D_MODEL), lambda i: (i, 0)),
        out_shape=jax.ShapeDtypeStruct((t, D_MODEL), F32),
        compiler_params=pltpu.CompilerParams(
            dimension_semantics=("arbitrary",), vmem_limit_bytes=VMEM_LIMIT_BYTES),
        name="oproj_mlp_final" if final else "oproj_mlp",
    )(h, o, wo, g, w1_all, w2_all, gf)


def _gla_in_kernel(x_ref, g_ref, w_ref, q_ref, k_ref, v_ref, r_ref, z_ref):
    tm = x_ref.shape[0]
    v0 = 2 * GLA_QK
    r0 = v0 + GLA_V
    z0 = r0 + GLA_V
    xn = _rms(x_ref[...], g_ref[...]).astype(BF16)
    qk = jnp.dot(xn, w_ref[:, :v0], preferred_element_type=F32)
    q_ref[...] = (qk[:, :GLA_QK] * (GLA_DK ** -0.5)).astype(BF16)
    k_ref[...] = qk[:, GLA_QK:].astype(BF16)
    v_ref[...] = jnp.dot(xn, w_ref[:, v0:r0], preferred_element_type=F32).astype(BF16)
    r_ref[...] = jnp.dot(xn, w_ref[:, r0:z0], preferred_element_type=F32).astype(BF16)
    codes = jnp.dot(xn, w_ref[:, z0:], preferred_element_type=F32)
    z_ref[...] = jnp.concatenate(
        [codes, jnp.zeros((tm, LANES - 2 * GATE_RANK), F32)], axis=1)


def _gla_inproj(h, g, w_in):
    t = h.shape[0]
    tm = ROW_TILE
    row = lambda i: (i, 0)
    return pl.pallas_call(
        _gla_in_kernel,
        grid=(t // tm,),
        in_specs=[
            pl.BlockSpec((tm, D_MODEL), row),
            _const_spec((1, D_MODEL)),
            _const_spec(w_in.shape),
        ],
        out_specs=[
            pl.BlockSpec((tm, GLA_QK), row),
            pl.BlockSpec((tm, GLA_QK), row),
            pl.BlockSpec((tm, GLA_V), row),
            pl.BlockSpec((tm, GLA_V), row),
            pl.BlockSpec((tm, LANES), row),
        ],
        out_shape=[
            jax.ShapeDtypeStruct((t, GLA_QK), BF16),
            jax.ShapeDtypeStruct((t, GLA_QK), BF16),
            jax.ShapeDtypeStruct((t, GLA_V), BF16),
            jax.ShapeDtypeStruct((t, GLA_V), BF16),
            jax.ShapeDtypeStruct((t, LANES), F32),
        ],
        compiler_params=pltpu.CompilerParams(
            dimension_semantics=("arbitrary",), vmem_limit_bytes=VMEM_LIMIT_BYTES),
        name="gla_inproj",
    )(h, g, w_in)


def _gla_kernel(q_ref, k_ref, v_ref, r_ref, z_ref, wcat_ref, bias_ref, og_ref, cum_ref, out_ref,
                lhs_scr, kv_scr, dec_scr, sums_scr):
    seq = q_ref.shape[0]
    c64 = GLA_CHUNK
    tile = GLA_TILE
    n_tiles = seq // tile
    grp = GLA_GROUP
    grp_rows = grp * tile
    dk = GLA_DK

    ridx = lax.broadcasted_iota(jnp.int32, (tile, tile), 0)
    cidx = lax.broadcasted_iota(jnp.int32, (tile, tile), 1)
    same_chunk = (ridx // c64) == (cidx // c64)
    fwd_diag = same_chunk & (ridx >= cidx)
    fwd_cross = (ridx >= c64) & (cidx < c64)
    bwd_diag = same_chunk & (ridx <= cidx)
    bwd_cross = (ridx < c64) & (cidx >= c64)
    second = lax.broadcasted_iota(jnp.int32, (tile, dk), 0) >= c64
    zeros_td = jnp.zeros((tile, dk), BF16)
    zeros_2td = jnp.zeros((2 * tile, dk), BF16)

    def gate_sums(g):
        goff = pl.multiple_of(g * grp_rows, grp_rows)
        z_hi, z_lo = _hi_lo(z_ref[pl.ds(goff, grp_rows), :])
        zcat = jnp.concatenate([z_hi, z_lo, z_hi], axis=1)
        logit = jnp.dot(zcat, wcat_ref[...], preferred_element_type=F32) + bias_ref[...]
        lg2 = (jnp.minimum(logit, 0.0) - jnp.log(1.0 + jnp.exp(-jnp.abs(logit)))) * (LOG2E / GATE_TAU)

        sums = []
        for t in range(grp):
            lg_hi, lg_lo = _hi_lo(lg2[t * tile:(t + 1) * tile])
            rhs = jnp.concatenate([
                jnp.concatenate([lg_hi[:, :dk], zeros_td], axis=1),
                jnp.concatenate([lg_lo[:, :dk], zeros_td], axis=1),
                jnp.concatenate([zeros_td, lg_hi[:, dk:]], axis=1),
                jnp.concatenate([zeros_td, lg_lo[:, dk:]], axis=1)], axis=0)
            sums.append(jnp.dot(cum_ref[...], rhs, preferred_element_type=F32))
        return sums

    def tile_operands(g, sums):
        goff = pl.multiple_of(g * grp_rows, grp_rows)
        scores = []
        kvs = []
        for t in range(grp):
            rows = pl.ds(goff + t * tile, tile)
            qf = q_ref[rows, :].astype(F32)
            kf = k_ref[rows, :].astype(F32)
            vv = v_ref[rows, :]
            b_f = sums[t][:, :dk]
            b_b = sums[t][:, dk:]
            tf0, tf1 = b_f[c64 - 1:c64], b_f[tile - 1:tile]
            tb0, tb1 = b_b[0:1], b_b[c64:c64 + 1]
            qe_f32 = qf * jnp.exp2(b_f)
            qe_f = qe_f32.astype(BF16)
            ke_f = (kf * jnp.exp2(-b_f)).astype(BF16)
            kd_f32 = kf * jnp.exp2(jnp.where(second, tf1, tf0) - b_f)
            qe_b32 = qf * jnp.exp2(b_b)
            qe_b = qe_b32.astype(BF16)
            ke_b = (kf * jnp.exp2(-b_b)).astype(BF16)
            kd_b32 = kf * jnp.exp2(jnp.where(second, tb1, tb0) - b_b)
            qt_f = (qe_f32 * jnp.where(second, jnp.exp2(tf0), 1.0)).astype(BF16)
            qt_b = (qe_b32 * jnp.where(second, 1.0, jnp.exp2(tb1))).astype(BF16)
            kt_f = kd_f32 * jnp.where(second, 1.0, jnp.exp2(tf1))
            kt_b = kd_b32 * jnp.where(second, jnp.exp2(tb0), 1.0)

            keys = jnp.concatenate([
                jnp.concatenate([ke_f, kd_f32.astype(BF16)], axis=0),
                jnp.concatenate([ke_b, kd_b32.astype(BF16)], axis=0)], axis=0)
            keys = jnp.concatenate([
                jnp.concatenate([keys[:2 * tile], zeros_2td], axis=1),
                jnp.concatenate([zeros_2td, keys[2 * tile:]], axis=1)], axis=0)
            sc = lax.dot_general(jnp.concatenate([qe_f, qe_b], axis=1), keys,
                                 (((1,), (1,)), ((), ())), preferred_element_type=F32)
            scores.append((sc, qt_f, qt_b))

            kt_both = jnp.concatenate([
                jnp.concatenate([kt_f.T.astype(BF16), zeros_td], axis=1),
                jnp.concatenate([zeros_td, kt_b.T.astype(BF16)], axis=1)], axis=0)
            kvs.append(jnp.dot(kt_both, jnp.concatenate([vv, vv], axis=0),
                               preferred_element_type=F32))
            idx = g * grp + t
            dec_scr[0, idx] = jnp.broadcast_to(jnp.exp2(tf0 + tf1), (dk, dk)).T
            dec_scr[1, idx] = jnp.broadcast_to(jnp.exp2(tb0 + tb1), (dk, dk)).T

        for t in range(grp):
            rows = pl.ds(goff + t * tile, tile)
            sc, qt_f, qt_b = scores[t]
            a_sum = (jnp.where(fwd_diag, sc[:, :tile], 0.0)
                     + jnp.where(fwd_cross, sc[:, tile:2 * tile], 0.0)
                     + jnp.where(bwd_diag, sc[:, 2 * tile:3 * tile], 0.0)
                     + jnp.where(bwd_cross, sc[:, 3 * tile:], 0.0))
            lhs_scr[rows, :] = jnp.concatenate([a_sum.astype(BF16), qt_f, qt_b], axis=1)
            idx = g * grp + t
            kv_scr[0, idx] = kvs[t][:dk]
            kv_scr[1, idx] = kvs[t][dk:]

    n_groups = n_tiles // grp

    def load_sums():
        return [sums_scr[t * tile:(t + 1) * tile, :] for t in range(grp)]

    def store_sums(sums):
        for t in range(grp):
            sums_scr[t * tile:(t + 1) * tile, :] = sums[t]

    store_sums(gate_sums(0))

    def phase_a(j, carry):
        g0 = 2 * j
        sums0 = load_sums()
        sums1 = gate_sums(g0 + 1)
        tile_operands(g0, sums0)
        sums2 = gate_sums(jnp.minimum(g0 + 2, n_groups - 1))
        tile_operands(g0 + 1, sums1)
        store_sums(sums2)
        return carry

    lax.fori_loop(0, n_groups // 2, phase_a, 0)

    for d in range(2):
        def step(i, run, d=d):
            t = i if d == 0 else n_tiles - 1 - i
            inc = kv_scr[d, t]
            kv_scr[d, t] = run
            dec = dec_scr[d, t]
            return jnp.concatenate([dec, dec], axis=1) * run + inc

        lax.fori_loop(0, n_tiles, step, jnp.zeros((dk, GLA_DV), F32))

    def phase_c(g, carry):
        goff = pl.multiple_of(g * grp_rows, grp_rows)
        outs = []
        for t in range(grp):
            rows = pl.ds(goff + t * tile, tile)
            idx = g * grp + t
            rhs = jnp.concatenate([v_ref[rows, :], kv_scr[0, idx].astype(BF16),
                                   kv_scr[1, idx].astype(BF16)], axis=0)
            outs.append(jnp.dot(lhs_scr[rows, :], rhs, preferred_element_type=F32))
        for t in range(grp):
            rows = pl.ds(goff + t * tile, tile)
            on = _rms(outs[t], og_ref[...])
            rr = r_ref[rows, :].astype(F32)
            out_ref[rows, :] = (on * (rr * jax.nn.sigmoid(rr))).astype(BF16)
        return carry

    lax.fori_loop(0, n_tiles // grp, phase_c, 0)


def _gla_scan(q, k, v, r, z, wcat, bias, og, cum, batch, seq):
    t = q.shape[0]
    n_tiles = seq // GLA_TILE
    bh = lambda b, h: (b, h)
    return pl.pallas_call(
        _gla_kernel,
        grid=(batch, GLA_HEADS),
        in_specs=[
            pl.BlockSpec((seq, GLA_DK), bh),
            pl.BlockSpec((seq, GLA_DK), bh),
            pl.BlockSpec((seq, GLA_DV), bh),
            pl.BlockSpec((seq, GLA_DV), bh),
            pl.BlockSpec((seq, LANES), lambda b, h: (b, 0)),
            pl.BlockSpec((3 * LANES, 2 * GLA_DK), lambda b, h: (0, h)),
            pl.BlockSpec((1, 2 * GLA_DK), lambda b, h: (0, h)),
            _const_spec((1, GLA_DV)),
            _const_spec(cum.shape),
        ],
        out_specs=pl.BlockSpec((seq, GLA_DV), bh),
        out_shape=jax.ShapeDtypeStruct((t, GLA_V), BF16),
        scratch_shapes=[
            pltpu.VMEM((seq, GLA_TILE + 2 * GLA_DK), BF16),
            pltpu.VMEM((2, n_tiles, GLA_DK, GLA_DV), F32),
            pltpu.VMEM((2, n_tiles, GLA_DK, LANES), F32),
            pltpu.VMEM((GLA_GROUP * GLA_TILE, 2 * GLA_DK), F32),
        ],
        compiler_params=pltpu.CompilerParams(
            dimension_semantics=("arbitrary", "arbitrary"),
            vmem_limit_bytes=VMEM_LIMIT_BYTES),
        name="gla_scan",
    )(q, k, v, r, z, wcat, bias, og, cum)


def _rope_tables(seq):
    pos = jnp.arange(seq, dtype=jnp.int32)
    half = HEAD_DIM // 2
    inv = ROPE_THETA ** (-jnp.arange(0, half, 2, dtype=F32) / half)
    ang_r = (pos // GRID_W).astype(F32)[:, None] * inv[None, :]
    ang_c = (pos % GRID_W).astype(F32)[:, None] * inv[None, :]
    cos_h = jnp.concatenate([jnp.cos(ang_r)] * 2 + [jnp.cos(ang_c)] * 2, axis=-1)
    sin_h = jnp.concatenate([-jnp.sin(ang_r), jnp.sin(ang_r), -jnp.sin(ang_c), jnp.sin(ang_c)], axis=-1)
    return jnp.tile(cos_h, (1, 2)), jnp.tile(sin_h, (1, 2))


def _gla_cumsum_matrix():
    n = GLA_TILE
    r = jnp.arange(n)[:, None]
    c = jnp.arange(n)[None, :]
    same = (r // GLA_CHUNK) == (c // GLA_CHUNK)
    prefix = (same & (c <= r)).astype(BF16)
    suffix = (same & (c >= r)).astype(BF16)
    return jnp.concatenate([prefix, prefix, suffix, suffix], axis=1)


def kernel(x, norm_mix, norm_mlp, attn_w_qkv, attn_q_norm, attn_k_norm, attn_w_o,
           gla_w_in, gla_w_gate_up, gla_b_gate, gla_out_norm, gla_w_o,
           mlp_w_in, mlp_w_out, final_norm):
    batch, seq, d = x.shape
    t = batch * seq
    h = x.reshape(t, d)
    row = lambda a: a.reshape(1, -1).astype(F32)

    cos, sin = _rope_tables(seq)
    gidx = jnp.arange(LANES)
    gsum = (gidx[:, None] // HEAD_DIM == gidx[None, :] // HEAD_DIM).astype(BF16)
    q_gain = jnp.tile(attn_q_norm[0], 2).reshape(1, LANES) * (LOG2E / math.sqrt(HEAD_DIM))
    k_gain = jnp.tile(attn_k_norm[0], 2).reshape(1, LANES)
    q, kt, v = _qkv_proj(h, row(norm_mix[0]), attn_w_qkv[0].astype(BF16), q_gain, k_gain,
                         cos, sin, gsum, batch, seq)
    o = _attention(q, kt, v, batch, seq)
    w1_all = mlp_w_in.astype(BF16)
    w2_all = mlp_w_out.astype(BF16)
    h = _oproj_mlp(h, o, attn_w_o[0].astype(BF16), row(norm_mlp[0]),
                   w1_all, w2_all, 0, row(final_norm), False)

    gq, gk, gv, gr, gz = _gla_inproj(h, row(norm_mix[1]), gla_w_in[0].astype(BF16))
    wup = jnp.zeros((2, LANES, GLA_QK), F32)
    wup = wup.at[0, :GATE_RANK].set(gla_w_gate_up[0, 0])
    wup = wup.at[1, GATE_RANK:2 * GATE_RANK].set(gla_w_gate_up[0, 1])
    wup = wup.reshape(2, LANES, GLA_HEADS, GLA_DK).transpose(1, 2, 0, 3).reshape(LANES, 2 * GLA_QK)
    wup_hi, wup_lo = _hi_lo(wup)
    wcat = jnp.concatenate([wup_hi, wup_hi, wup_lo], axis=0)
    b_gate = gla_b_gate[0].reshape(2, GLA_HEADS, GLA_DK).transpose(1, 0, 2).reshape(1, 2 * GLA_QK)
    go = _gla_scan(gq, gk, gv, gr, gz, wcat, b_gate,
                   row(gla_out_norm[0]), _gla_cumsum_matrix(), batch, seq)
    h = _oproj_mlp(h, go, gla_w_o[0].astype(BF16), row(norm_mlp[1]),
                   w1_all, w2_all, 1, row(final_norm), True)
    return h.reshape(batch, seq, d)
```

```python
import functools
import math

import jax
import jax.numpy as jnp
from jax import lax
from jax.experimental import pallas as pl
from jax.experimental.pallas import tpu as pltpu

F32 = jnp.float32
BF16 = jnp.bfloat16

D_MODEL = 1024
GRID_W = 64
N_HEADS = 16
N_KV_HEADS = 4
HEAD_DIM = 64
GROUP = N_HEADS // N_KV_HEADS
ROPE_THETA = 10000.0
GLA_HEADS = 4
GLA_DK = 128
GLA_DV = 256
GLA_QK = GLA_HEADS * GLA_DK
GLA_V = GLA_HEADS * GLA_DV
GATE_RANK = 16
GATE_TAU = 16.0
D_FF = 4 * D_MODEL
EPS = 1e-6

LANES = 128
VMEM_LIMIT_BYTES = 56 * 1024 * 1024

ROW_TILE = 512
QKV_SPLIT = 4
ATTN_TQ = 512
ATTN_TK = 512
FF_TILE = 1024
GLA_CHUNK = 64
GLA_TILE = 2 * GLA_CHUNK
GLA_GROUP = 4
LOG2E = 1.4426950408889634


def _const_spec(shape):
    nd = len(shape)
    return pl.BlockSpec(shape, lambda *_: (0,) * nd, pipeline_mode=pl.Buffered(1))


def _rms(x, gain):
    ms = jnp.mean(x * x, axis=-1, keepdims=True)
    return x * lax.rsqrt(ms + EPS) * gain


def _hi_lo(x_f32):
    hi = x_f32.astype(BF16)
    lo = (x_f32 - hi.astype(F32)).astype(BF16)
    return hi, lo


def _qkv_kernel(x_ref, g_ref, w_ref, qg_ref, kg_ref, cos_ref, sin_ref, gsum_ref,
                q_ref, kt_ref, v_ref):
    tm = x_ref.shape[0]
    sub = tm // QKV_SPLIT
    lane = lax.broadcasted_iota(jnp.int32, (sub, LANES), 1)
    first_half = (lane % 32) < 16
    gsum = gsum_ref[...]
    nq = N_HEADS * HEAD_DIM
    nk = N_KV_HEADS * HEAD_DIM

    for r in range(QKV_SPLIT):
        rows = slice(r * sub, (r + 1) * sub)
        xn = _rms(x_ref[rows, :], g_ref[...]).astype(BF16)
        cos = cos_ref[rows, :]
        sin = sin_ref[rows, :]

        def norm_rope(c4, gain):
            ms = jnp.dot((c4 * c4).astype(BF16), gsum, preferred_element_type=F32)
            halves = []
            for hf in range(2):
                sl = slice(hf * LANES, (hf + 1) * LANES)
                cn = c4[:, sl] * lax.rsqrt(ms[:, sl] + EPS) * gain
                rot = jnp.where(first_half, pltpu.roll(cn, LANES - 16, 1), pltpu.roll(cn, 16, 1))
                halves.append(cn * cos + rot * sin)
            return halves

        y = jnp.dot(xn, w_ref[...], preferred_element_type=F32)
        for j in range(nq // (2 * LANES)):
            for hf, qh in enumerate(norm_rope(y[:, 2 * j * LANES:2 * (j + 1) * LANES], qg_ref[...])):
                col = (2 * j + hf) * LANES
                q_ref[rows, col:col + LANES] = qh.astype(BF16)
        for j, kh in enumerate(norm_rope(y[:, nq:nq + nk], kg_ref[...])):
            kt = kh.T
            kt_ref[0, 2 * j, :, rows] = kt[:HEAD_DIM].astype(BF16)
            kt_ref[0, 2 * j + 1, :, rows] = kt[HEAD_DIM:].astype(BF16)
        for g in range(N_KV_HEADS):
            c0 = nq + nk + (g // 2) * LANES
            pair = y[:, c0:c0 + LANES]
            if g % 2:
                pair = pltpu.roll(pair, HEAD_DIM, 1)
            v_ref[rows, g * LANES:(g + 1) * LANES] = jnp.where(lane < HEAD_DIM, pair, 1.0).astype(BF16)


def _qkv_proj(x2, g, w, qg, kg, cos, sin, gsum, batch, seq):
    t = x2.shape[0]
    tm = ROW_TILE
    spb = seq // tm
    nkv = N_KV_HEADS * LANES
    return pl.pallas_call(
        _qkv_kernel,
        grid=(t // tm,),
        in_specs=[
            pl.BlockSpec((tm, D_MODEL), lambda i: (i, 0)),
            _const_spec((1, D_MODEL)),
            _const_spec(w.shape),
            _const_spec((1, LANES)),
            _const_spec((1, LANES)),
            pl.BlockSpec((tm, LANES), lambda i: (i % spb, 0)),
            pl.BlockSpec((tm, LANES), lambda i: (i % spb, 0)),
            _const_spec(gsum.shape),
        ],
        out_specs=[
            pl.BlockSpec((tm, N_HEADS * HEAD_DIM), lambda i: (i, 0)),
            pl.BlockSpec((1, N_KV_HEADS, HEAD_DIM, tm), lambda i: (i // spb, 0, 0, i % spb)),
            pl.BlockSpec((tm, nkv), lambda i: (i, 0)),
        ],
        out_shape=[
            jax.ShapeDtypeStruct((t, N_HEADS * HEAD_DIM), BF16),
            jax.ShapeDtypeStruct((batch, N_KV_HEADS, HEAD_DIM, seq), BF16),
            jax.ShapeDtypeStruct((t, nkv), BF16),
        ],
        compiler_params=pltpu.CompilerParams(
            dimension_semantics=("arbitrary",), vmem_limit_bytes=VMEM_LIMIT_BYTES),
        name="qkv_proj",
    )(x2, g, w, qg, kg, cos, sin, gsum)


def _attn_kernel(q_ref, kt_ref, v_ref, o_ref):
    tq = q_ref.shape[0]
    seq = v_ref.shape[0]
    q4 = jnp.concatenate(
        [q_ref[:, h * HEAD_DIM:(h + 1) * HEAD_DIM] for h in range(GROUP)], axis=0)
    m = None
    acc = None
    for c in range(seq // ATTN_TK):
        ks = slice(c * ATTN_TK, (c + 1) * ATTN_TK)
        s = jnp.dot(q4, kt_ref[0, 0, :, ks], preferred_element_type=F32)
        m_c = jnp.max(s, axis=-1, keepdims=True)
        m_new = m_c if m is None else jnp.maximum(m, m_c)
        p = jnp.exp2((s - m_new).astype(BF16))
        pv = jnp.dot(p, v_ref[ks, :], preferred_element_type=F32)
        acc = pv if acc is None else acc * jnp.exp2(m - m_new) + pv
        m = m_new
    out = acc * (1.0 / pltpu.roll(acc, HEAD_DIM, 1))
    lane = lax.broadcasted_iota(jnp.int32, (tq, LANES), 1)
    for hp in range(GROUP // 2):
        even = out[(2 * hp) * tq:(2 * hp + 1) * tq]
        odd = pltpu.roll(out[(2 * hp + 1) * tq:(2 * hp + 2) * tq], HEAD_DIM, 1)
        o_ref[:, hp * LANES:(hp + 1) * LANES] = jnp.where(lane < HEAD_DIM, even, odd).astype(BF16)


def _attention(q, kt, v, batch, seq):
    t = q.shape[0]
    tq = ATTN_TQ
    nq = seq // tq
    grp_w = GROUP * HEAD_DIM
    return pl.pallas_call(
        _attn_kernel,
        grid=(batch, N_KV_HEADS, nq),
        in_specs=[
            pl.BlockSpec((tq, grp_w), lambda b, g, i: (b * nq + i, g)),
            pl.BlockSpec((1, 1, HEAD_DIM, seq), lambda b, g, i: (b, g, 0, 0)),
            pl.BlockSpec((seq, LANES), lambda b, g, i: (b, g)),
        ],
        out_specs=pl.BlockSpec((tq, grp_w), lambda b, g, i: (b * nq + i, g)),
        out_shape=jax.ShapeDtypeStruct((t, N_HEADS * HEAD_DIM), BF16),
        compiler_params=pltpu.CompilerParams(
            dimension_semantics=("arbitrary", "arbitrary", "arbitrary"),
            vmem_limit_bytes=VMEM_LIMIT_BYTES),
        name="attention",
    )(q, kt, v)


def _oproj_mlp_kernel(h_ref, o_ref, wo_ref, g_ref, w1_ref, w2_ref, gf_ref, out_ref, *, final):
    h1 = h_ref[...] + jnp.dot(o_ref[...], wo_ref[...], preferred_element_type=F32)
    hn = _rms(h1, g_ref[...]).astype(BF16)
    acc = h1
    for c in range(D_FF // FF_TILE):
        sl = slice(c * FF_TILE, (c + 1) * FF_TILE)
        u = jnp.dot(hn, w1_ref[:, sl], preferred_element_type=F32)
        a = jnp.square(jnp.maximum(u, 0.0)).astype(BF16)
        acc = acc + jnp.dot(a, w2_ref[sl, :], preferred_element_type=F32)
    if final:
        acc = _rms(acc, gf_ref[...])
    out_ref[...] = acc


def _oproj_mlp(h, o, wo, g, w1_all, w2_all, layer, gf, final):
    t = h.shape[0]
    tm = ROW_TILE
    layer_spec = lambda a: pl.BlockSpec((None,) + a.shape[1:], lambda i: (layer, 0, 0),
                                        pipeline_mode=pl.Buffered(1))
    return pl.pallas_call(
        functools.partial(_oproj_mlp_kernel, final=final),
        grid=(t // tm,),
        in_specs=[
            pl.BlockSpec((tm, D_MODEL), lambda i: (i, 0)),
            pl.BlockSpec((tm, o.shape[1]), lambda i: (i, 0)),
            _const_spec(wo.shape),
            _const_spec((1, D_MODEL)),
            layer_spec(w1_all),
            layer_spec(w2_all),
            _const_spec((1, D_MODEL)),
        ],
        out_specs=pl.BlockSpec((tm, D_MODEL), lambda i: (i, 0)),
        out_shape=jax.ShapeDtypeStruct((t, D_MODEL), F32),
        compiler_params=pltpu.CompilerParams(
            dimension_semantics=("arbitrary",), vmem_limit_bytes=VMEM_LIMIT_BYTES),
        name="oproj_mlp_final" if final else "oproj_mlp",
    )(h, o, wo, g, w1_all, w2_all, gf)


def _gla_in_kernel(x_ref, g_ref, w_ref, q_ref, k_ref, v_ref, r_ref, z_ref):
    tm = x_ref.shape[0]
    v0 = 2 * GLA_QK
    r0 = v0 + GLA_V
    z0 = r0 + GLA_V
    xn = _rms(x_ref[...], g_ref[...]).astype(BF16)
    qk = jnp.dot(xn, w_ref[:, :v0], preferred_element_type=F32)
    q_ref[...] = (qk[:, :GLA_QK] * (GLA_DK ** -0.5)).astype(BF16)
    k_ref[...] = qk[:, GLA_QK:].astype(BF16)
    v_ref[...] = jnp.dot(xn, w_ref[:, v0:r0], preferred_element_type=F32).astype(BF16)
    r_ref[...] = jnp.dot(xn, w_ref[:, r0:z0], preferred_element_type=F32).astype(BF16)
    codes = jnp.dot(xn, w_ref[:, z0:], preferred_element_type=F32)
    z_ref[...] = jnp.concatenate(
        [codes, jnp.zeros((tm, LANES - 2 * GATE_RANK), F32)], axis=1)


def _gla_inproj(h, g, w_in):
    t = h.shape[0]
    tm = ROW_TILE
    row = lambda i: (i, 0)
    return pl.pallas_call(
        _gla_in_kernel,
        grid=(t // tm,),
        in_specs=[
            pl.BlockSpec((tm, D_MODEL), row),
            _const_spec((1, D_MODEL)),
            _const_spec(w_in.shape),
        ],
        out_specs=[
            pl.BlockSpec((tm, GLA_QK), row),
            pl.BlockSpec((tm, GLA_QK), row),
            pl.BlockSpec((tm, GLA_V), row),
            pl.BlockSpec((tm, GLA_V), row),
            pl.BlockSpec((tm, LANES), row),
        ],
        out_shape=[
            jax.ShapeDtypeStruct((t, GLA_QK), BF16),
            jax.ShapeDtypeStruct((t, GLA_QK), BF16),
            jax.ShapeDtypeStruct((t, GLA_V), BF16),
            jax.ShapeDtypeStruct((t, GLA_V), BF16),
            jax.ShapeDtypeStruct((t, LANES), F32),
        ],
        compiler_params=pltpu.CompilerParams(
            dimension_semantics=("arbitrary",), vmem_limit_bytes=VMEM_LIMIT_BYTES),
        name="gla_inproj",
    )(h, g, w_in)


def _gla_kernel(q_ref, k_ref, v_ref, r_ref, z_ref, wcat_ref, bias_ref, og_ref, cum_ref, out_ref,
                lhs_scr, kv_scr, dec_scr, stage_scr):
    seq = q_ref.shape[0]
    c64 = GLA_CHUNK
    tile = GLA_TILE
    n_tiles = seq // tile
    grp = GLA_GROUP
    grp_rows = grp * tile
    dk = GLA_DK

    ridx = lax.broadcasted_iota(jnp.int32, (tile, tile), 0)
    cidx = lax.broadcasted_iota(jnp.int32, (tile, tile), 1)
    same_chunk = (ridx // c64) == (cidx // c64)
    fwd_diag = same_chunk & (ridx >= cidx)
    fwd_cross = (ridx >= c64) & (cidx < c64)
    bwd_diag = same_chunk & (ridx <= cidx)
    bwd_cross = (ridx < c64) & (cidx >= c64)
    second = lax.broadcasted_iota(jnp.int32, (tile, dk), 0) >= c64
    zeros_td = jnp.zeros((tile, dk), BF16)
    zeros_2td = jnp.zeros((2 * tile, dk), BF16)

    def gate_sums(g):
        goff = pl.multiple_of(g * grp_rows, grp_rows)
        z_hi, z_lo = _hi_lo(z_ref[pl.ds(goff, grp_rows), :])
        zcat = jnp.concatenate([z_hi, z_lo, z_hi], axis=1)
        logit = jnp.dot(zcat, wcat_ref[...], preferred_element_type=F32) + bias_ref[...]
        lg2 = (jnp.minimum(logit, 0.0) - jnp.log(1.0 + jnp.exp(-jnp.abs(logit)))) * (LOG2E / GATE_TAU)

        sums = []
        for t in range(grp):
            lg_hi, lg_lo = _hi_lo(lg2[t * tile:(t + 1) * tile])
            rhs = jnp.concatenate([
                jnp.concatenate([lg_hi[:, :dk], zeros_td], axis=1),
                jnp.concatenate([lg_lo[:, :dk], zeros_td], axis=1),
                jnp.concatenate([zeros_td, lg_hi[:, dk:]], axis=1),
                jnp.concatenate([zeros_td, lg_lo[:, dk:]], axis=1)], axis=0)
            sums.append(jnp.dot(cum_ref[...], rhs, preferred_element_type=F32))
        return sums

    def tile_operands(g, sums):
        goff = pl.multiple_of(g * grp_rows, grp_rows)
        scores = []
        kvs = []
        for t in range(grp):
            rows = pl.ds(goff + t * tile, tile)
            qf = q_ref[rows, :].astype(F32)
            kf = k_ref[rows, :].astype(F32)
            vv = v_ref[rows, :]
            b_f = sums[t][:, :dk]
            b_b = sums[t][:, dk:]
            tf0, tf1 = b_f[c64 - 1:c64], b_f[tile - 1:tile]
            tb0, tb1 = b_b[0:1], b_b[c64:c64 + 1]
            qe_f32 = qf * jnp.exp2(b_f)
            qe_f = qe_f32.astype(BF16)
            ke_f = (kf * jnp.exp2(-b_f)).astype(BF16)
            kd_f32 = kf * jnp.exp2(jnp.where(second, tf1, tf0) - b_f)
            qe_b32 = qf * jnp.exp2(b_b)
            qe_b = qe_b32.astype(BF16)
            ke_b = (kf * jnp.exp2(-b_b)).astype(BF16)
            kd_b32 = kf * jnp.exp2(jnp.where(second, tb1, tb0) - b_b)
            qt_f = (qe_f32 * jnp.where(second, jnp.exp2(tf0), 1.0)).astype(BF16)
            qt_b = (qe_b32 * jnp.where(second, 1.0, jnp.exp2(tb1))).astype(BF16)
            kt_f = kd_f32 * jnp.where(second, 1.0, jnp.exp2(tf1))
            kt_b = kd_b32 * jnp.where(second, jnp.exp2(tb0), 1.0)

            keys = jnp.concatenate([
                jnp.concatenate([ke_f, kd_f32.astype(BF16)], axis=0),
                jnp.concatenate([ke_b, kd_b32.astype(BF16)], axis=0)], axis=0)
            keys = jnp.concatenate([
                jnp.concatenate([keys[:2 * tile], zeros_2td], axis=1),
                jnp.concatenate([zeros_2td, keys[2 * tile:]], axis=1)], axis=0)
            sc = lax.dot_general(jnp.concatenate([qe_f, qe_b], axis=1), keys,
                                 (((1,), (1,)), ((), ())), preferred_element_type=F32)
            scores.append((sc, qt_f, qt_b))

            kt_both = jnp.concatenate([
                jnp.concatenate([kt_f.T.astype(BF16), zeros_td], axis=1),
                jnp.concatenate([zeros_td, kt_b.T.astype(BF16)], axis=1)], axis=0)
            kvs.append(jnp.dot(kt_both, jnp.concatenate([vv, vv], axis=0),
                               preferred_element_type=F32))
            idx = g * grp + t
            dec_scr[0, idx] = jnp.broadcast_to(jnp.exp2(tf0 + tf1), (dk, dk)).T
            dec_scr[1, idx] = jnp.broadcast_to(jnp.exp2(tb0 + tb1), (dk, dk)).T

        for t in range(grp):
            rows = pl.ds(goff + t * tile, tile)
            sc, qt_f, qt_b = scores[t]
            a_sum = (jnp.where(fwd_diag, sc[:, :tile], 0.0)
                     + jnp.where(fwd_cross, sc[:, tile:2 * tile], 0.0)
                     + jnp.where(bwd_diag, sc[:, 2 * tile:3 * tile], 0.0)
                     + jnp.where(bwd_cross, sc[:, 3 * tile:], 0.0))
            lhs_scr[rows, :] = jnp.concatenate([a_sum.astype(BF16), qt_f, qt_b], axis=1)
            idx = g * grp + t
            kv_scr[0, idx] = kvs[t][:dk]
            kv_scr[1, idx] = kvs[t][dk:]

    n_groups = n_tiles // grp

    def run_two_stage(first, second):
        def load():
            return [stage_scr[t * tile:(t + 1) * tile, :] for t in range(grp)]

        def store(vals):
            for t in range(grp):
                stage_scr[t * tile:(t + 1) * tile, :] = vals[t]

        store(first(0))

        def step(j, carry):
            g0 = 2 * j
            vals0 = load()
            vals1 = first(g0 + 1)
            second(g0, vals0)
            vals2 = first(jnp.minimum(g0 + 2, n_groups - 1))
            second(g0 + 1, vals1)
            store(vals2)
            return carry

        lax.fori_loop(0, n_groups // 2, step, 0)

    run_two_stage(gate_sums, tile_operands)

    for d in range(2):
        def step(i, run, d=d):
            t = i if d == 0 else n_tiles - 1 - i
            inc = kv_scr[d, t]
            kv_scr[d, t] = run
            dec = dec_scr[d, t]
            return jnp.concatenate([dec, dec], axis=1) * run + inc

        lax.fori_loop(0, n_tiles, step, jnp.zeros((dk, GLA_DV), F32))

    def out_dots(g):
        goff = pl.multiple_of(g * grp_rows, grp_rows)
        outs = []
        for t in range(grp):
            rows = pl.ds(goff + t * tile, tile)
            idx = g * grp + t
            rhs = jnp.concatenate([v_ref[rows, :], kv_scr[0, idx].astype(BF16),
                                   kv_scr[1, idx].astype(BF16)], axis=0)
            outs.append(jnp.dot(lhs_scr[rows, :], rhs, preferred_element_type=F32))
        return outs

    def norm_gate(g, outs):
        goff = pl.multiple_of(g * grp_rows, grp_rows)
        for t in range(grp):
            rows = pl.ds(goff + t * tile, tile)
            on = _rms(outs[t], og_ref[...])
            rr = r_ref[rows, :].astype(F32)
            out_ref[rows, :] = (on * (rr * jax.nn.sigmoid(rr))).astype(BF16)

    run_two_stage(out_dots, norm_gate)


def _gla_scan(q, k, v, r, z, wcat, bias, og, cum, batch, seq):
    t = q.shape[0]
    n_tiles = seq // GLA_TILE
    bh = lambda b, h: (b, h)
    return pl.pallas_call(
        _gla_kernel,
        grid=(batch, GLA_HEADS),
        in_specs=[
            pl.BlockSpec((seq, GLA_DK), bh),
            pl.BlockSpec((seq, GLA_DK), bh),
            pl.BlockSpec((seq, GLA_DV), bh),
            pl.BlockSpec((seq, GLA_DV), bh),
            pl.BlockSpec((seq, LANES), lambda b, h: (b, 0)),
            pl.BlockSpec((3 * LANES, 2 * GLA_DK), lambda b, h: (0, h)),
            pl.BlockSpec((1, 2 * GLA_DK), lambda b, h: (0, h)),
            _const_spec((1, GLA_DV)),
            _const_spec(cum.shape),
        ],
        out_specs=pl.BlockSpec((seq, GLA_DV), bh),
        out_shape=jax.ShapeDtypeStruct((t, GLA_V), BF16),
        scratch_shapes=[
            pltpu.VMEM((seq, GLA_TILE + 2 * GLA_DK), BF16),
            pltpu.VMEM((2, n_tiles, GLA_DK, GLA_DV), F32),
            pltpu.VMEM((2, n_tiles, GLA_DK, LANES), F32),
            pltpu.VMEM((GLA_GROUP * GLA_TILE, 2 * GLA_DK), F32),
        ],
        compiler_params=pltpu.CompilerParams(
            dimension_semantics=("arbitrary", "arbitrary"),
            vmem_limit_bytes=VMEM_LIMIT_BYTES),
        name="gla_scan",
    )(q, k, v, r, z, wcat, bias, og, cum)


def _rope_tables(seq):
    pos = jnp.arange(seq, dtype=jnp.int32)
    half = HEAD_DIM // 2
    inv = ROPE_THETA ** (-jnp.arange(0, half, 2, dtype=F32) / half)
    ang_r = (pos // GRID_W).astype(F32)[:, None] * inv[None, :]
    ang_c = (pos % GRID_W).astype(F32)[:, None] * inv[None, :]
    cos_h = jnp.concatenate([jnp.cos(ang_r)] * 2 + [jnp.cos(ang_c)] * 2, axis=-1)
    sin_h = jnp.concatenate([-jnp.sin(ang_r), jnp.sin(ang_r), -jnp.sin(ang_c), jnp.sin(ang_c)], axis=-1)
    return jnp.tile(cos_h, (1, 2)), jnp.tile(sin_h, (1, 2))


def _gla_cumsum_matrix():
    n = GLA_TILE
    r = jnp.arange(n)[:, None]
    c = jnp.arange(n)[None, :]
    same = (r // GLA_CHUNK) == (c // GLA_CHUNK)
    prefix = (same & (c <= r)).astype(BF16)
    suffix = (same & (c >= r)).astype(BF16)
    return jnp.concatenate([prefix, prefix, suffix, suffix], axis=1)


def kernel(x, norm_mix, norm_mlp, attn_w_qkv, attn_q_norm, attn_k_norm, attn_w_o,
           gla_w_in, gla_w_gate_up, gla_b_gate, gla_out_norm, gla_w_o,
           mlp_w_in, mlp_w_out, final_norm):
    batch, seq, d = x.shape
    t = batch * seq
    h = x.reshape(t, d)
    row = lambda a: a.reshape(1, -1).astype(F32)

    cos, sin = _rope_tables(seq)
    gidx = jnp.arange(2 * LANES)
    same_head = gidx[:, None] // HEAD_DIM == gidx[None, :] // HEAD_DIM
    gsum = jnp.where(same_head, 1.0 / HEAD_DIM, 0.0).astype(BF16)
    q_gain = jnp.tile(attn_q_norm[0], 2).reshape(1, LANES) * (LOG2E / math.sqrt(HEAD_DIM))
    k_gain = jnp.tile(attn_k_norm[0], 2).reshape(1, LANES)
    q, kt, v = _qkv_proj(h, row(norm_mix[0]), attn_w_qkv[0].astype(BF16), q_gain, k_gain,
                         cos, sin, gsum, batch, seq)
    o = _attention(q, kt, v, batch, seq)
    w1_all = mlp_w_in.astype(BF16)
    w2_all = mlp_w_out.astype(BF16)
    h = _oproj_mlp(h, o, attn_w_o[0].astype(BF16), row(norm_mlp[0]),
                   w1_all, w2_all, 0, row(final_norm), False)

    gq, gk, gv, gr, gz = _gla_inproj(h, row(norm_mix[1]), gla_w_in[0].astype(BF16))
    wup = jnp.zeros((2, LANES, GLA_QK), F32)
    wup = wup.at[0, :GATE_RANK].set(gla_w_gate_up[0, 0])
    wup = wup.at[1, GATE_RANK:2 * GATE_RANK].set(gla_w_gate_up[0, 1])
    wup = wup.reshape(2, LANES, GLA_HEADS, GLA_DK).transpose(1, 2, 0, 3).reshape(LANES, 2 * GLA_QK)
    wup_hi, wup_lo = _hi_lo(wup)
    wcat = jnp.concatenate([wup_hi, wup_hi, wup_lo], axis=0)
    b_gate = gla_b_gate[0].reshape(2, GLA_HEADS, GLA_DK).transpose(1, 0, 2).reshape(1, 2 * GLA_QK)
    go = _gla_scan(gq, gk, gv, gr, gz, wcat, b_gate,
                   row(gla_out_norm[0]), _gla_cumsum_matrix(), batch, seq)
    h = _oproj_mlp(h, go, gla_w_o[0].astype(BF16), row(norm_mlp[1]),
                   w1_all, w2_all, 1, row(final_norm), True)
    return h.reshape(batch, seq, d)
```

```python
import functools
import math

import jax
import jax.numpy as jnp
from jax import lax
from jax.experimental import pallas as pl
from jax.experimental.pallas import tpu as pltpu

F32 = jnp.float32
BF16 = jnp.bfloat16

D_MODEL = 1024
GRID_W = 64
N_HEADS = 16
N_KV_HEADS = 4
HEAD_DIM = 64
GROUP = N_HEADS // N_KV_HEADS
ROPE_THETA = 10000.0
GLA_HEADS = 4
GLA_DK = 128
GLA_DV = 256
GLA_QK = GLA_HEADS * GLA_DK
GLA_V = GLA_HEADS * GLA_DV
GATE_RANK = 16
GATE_TAU = 16.0
D_FF = 4 * D_MODEL
EPS = 1e-6

LANES = 128
VMEM_LIMIT_BYTES = 56 * 1024 * 1024

ROW_TILE = 1024
MLP_ROW_TILE = 1024
QKV_SPLIT = 8
ATTN_TQ = 512
ATTN_TK = 512
FF_TILE = 1024
GLA_CHUNK = 64
GLA_TILE = 2 * GLA_CHUNK
GLA_GROUP = 4
LOG2E = 1.4426950408889634


def _const_spec(shape):
    nd = len(shape)
    return pl.BlockSpec(shape, lambda *_: (0,) * nd, pipeline_mode=pl.Buffered(1))


def _rms(x, gain):
    ms = jnp.mean(x * x, axis=-1, keepdims=True)
    return x * lax.rsqrt(ms + EPS) * gain


def _hi_lo(x_f32):
    hi = x_f32.astype(BF16)
    lo = (x_f32 - hi.astype(F32)).astype(BF16)
    return hi, lo


def _qkv_kernel(x_ref, g_ref, w_ref, qg_ref, kg_ref, cos_ref, sin_ref, gsum_ref,
                q_ref, kt_ref, v_ref):
    tm = x_ref.shape[0]
    sub = tm // QKV_SPLIT
    lane = lax.broadcasted_iota(jnp.int32, (sub, LANES), 1)
    first_half = (lane % 32) < 16
    gsum = gsum_ref[...]
    nq = N_HEADS * HEAD_DIM
    nk = N_KV_HEADS * HEAD_DIM

    for r in range(QKV_SPLIT):
        rows = slice(r * sub, (r + 1) * sub)
        xn = _rms(x_ref[rows, :], g_ref[...]).astype(BF16)
        cos = cos_ref[rows, :]
        sin = sin_ref[rows, :]

        def norm_rope(c4, gain):
            ms = jnp.dot((c4 * c4).astype(BF16), gsum, preferred_element_type=F32)
            halves = []
            for hf in range(2):
                sl = slice(hf * LANES, (hf + 1) * LANES)
                cn = c4[:, sl] * lax.rsqrt(ms[:, sl] + EPS) * gain
                rot = jnp.where(first_half, pltpu.roll(cn, LANES - 16, 1), pltpu.roll(cn, 16, 1))
                halves.append(cn * cos + rot * sin)
            return halves

        y = jnp.dot(xn, w_ref[...], preferred_element_type=F32)
        for j in range(nq // (2 * LANES)):
            for hf, qh in enumerate(norm_rope(y[:, 2 * j * LANES:2 * (j + 1) * LANES], qg_ref[...])):
                col = (2 * j + hf) * LANES
                q_ref[rows, col:col + LANES] = qh.astype(BF16)
        for j, kh in enumerate(norm_rope(y[:, nq:nq + nk], kg_ref[...])):
            kt = kh.T
            kt_ref[0, 2 * j, :, rows] = kt[:HEAD_DIM].astype(BF16)
            kt_ref[0, 2 * j + 1, :, rows] = kt[HEAD_DIM:].astype(BF16)
        for g in range(N_KV_HEADS):
            c0 = nq + nk + (g // 2) * LANES
            pair = y[:, c0:c0 + LANES]
            if g % 2:
                pair = pltpu.roll(pair, HEAD_DIM, 1)
            v_ref[rows, g * LANES:(g + 1) * LANES] = jnp.where(lane < HEAD_DIM, pair, 1.0).astype(BF16)


def _qkv_proj(x2, g, w, qg, kg, cos, sin, gsum, batch, seq):
    t = x2.shape[0]
    tm = ROW_TILE
    spb = seq // tm
    nkv = N_KV_HEADS * LANES
    return pl.pallas_call(
        _qkv_kernel,
        grid=(t // tm,),
        in_specs=[
            pl.BlockSpec((tm, D_MODEL), lambda i: (i, 0)),
            _const_spec((1, D_MODEL)),
            _const_spec(w.shape),
            _const_spec((1, LANES)),
            _const_spec((1, LANES)),
            pl.BlockSpec((tm, LANES), lambda i: (i % spb, 0)),
            pl.BlockSpec((tm, LANES), lambda i: (i % spb, 0)),
            _const_spec(gsum.shape),
        ],
        out_specs=[
            pl.BlockSpec((tm, N_HEADS * HEAD_DIM), lambda i: (i, 0)),
            pl.BlockSpec((1, N_KV_HEADS, HEAD_DIM, tm), lambda i: (i // spb, 0, 0, i % spb)),
            pl.BlockSpec((tm, nkv), lambda i: (i, 0)),
        ],
        out_shape=[
            jax.ShapeDtypeStruct((t, N_HEADS * HEAD_DIM), BF16),
            jax.ShapeDtypeStruct((batch, N_KV_HEADS, HEAD_DIM, seq), BF16),
            jax.ShapeDtypeStruct((t, nkv), BF16),
        ],
        compiler_params=pltpu.CompilerParams(
            dimension_semantics=("arbitrary",), vmem_limit_bytes=VMEM_LIMIT_BYTES),
        name="qkv_proj",
    )(x2, g, w, qg, kg, cos, sin, gsum)


def _attn_kernel(q_ref, kt_ref, v_ref, o_ref):
    tq = q_ref.shape[0]
    seq = v_ref.shape[0]
    q4 = jnp.concatenate(
        [q_ref[:, h * HEAD_DIM:(h + 1) * HEAD_DIM] for h in range(GROUP)], axis=0)
    m = None
    acc = None
    for c in range(seq // ATTN_TK):
        ks = slice(c * ATTN_TK, (c + 1) * ATTN_TK)
        s = jnp.dot(q4, kt_ref[0, 0, :, ks], preferred_element_type=F32)
        m_c = jnp.max(s, axis=-1, keepdims=True)
        m_new = m_c if m is None else jnp.maximum(m, m_c)
        p = jnp.exp2((s - m_new).astype(BF16))
        pv = jnp.dot(p, v_ref[ks, :], preferred_element_type=F32)
        acc = pv if acc is None else acc * jnp.exp2(m - m_new) + pv
        m = m_new
    out = acc * (1.0 / pltpu.roll(acc, HEAD_DIM, 1))
    lane = lax.broadcasted_iota(jnp.int32, (tq, LANES), 1)
    for hp in range(GROUP // 2):
        even = out[(2 * hp) * tq:(2 * hp + 1) * tq]
        odd = pltpu.roll(out[(2 * hp + 1) * tq:(2 * hp + 2) * tq], HEAD_DIM, 1)
        o_ref[:, hp * LANES:(hp + 1) * LANES] = jnp.where(lane < HEAD_DIM, even, odd).astype(BF16)


def _attention(q, kt, v, batch, seq):
    t = q.shape[0]
    tq = ATTN_TQ
    nq = seq // tq
    grp_w = GROUP * HEAD_DIM
    return pl.pallas_call(
        _attn_kernel,
        grid=(batch, N_KV_HEADS, nq),
        in_specs=[
            pl.BlockSpec((tq, grp_w), lambda b, g, i: (b * nq + i, g)),
            pl.BlockSpec((1, 1, HEAD_DIM, seq), lambda b, g, i: (b, g, 0, 0)),
            pl.BlockSpec((seq, LANES), lambda b, g, i: (b, g)),
        ],
        out_specs=pl.BlockSpec((tq, grp_w), lambda b, g, i: (b * nq + i, g)),
        out_shape=jax.ShapeDtypeStruct((t, N_HEADS * HEAD_DIM), BF16),
        compiler_params=pltpu.CompilerParams(
            dimension_semantics=("arbitrary", "arbitrary", "arbitrary"),
            vmem_limit_bytes=VMEM_LIMIT_BYTES),
        name="attention",
    )(q, kt, v)


def _oproj_mlp_kernel(h_ref, o_ref, wo_ref, g_ref, w1_ref, w2_ref, gf_ref, out_ref, *, final):
    h1 = h_ref[...] + jnp.dot(o_ref[...], wo_ref[...], preferred_element_type=F32)
    hn = _rms(h1, g_ref[...]).astype(BF16)
    acc = h1
    for c in range(D_FF // FF_TILE):
        sl = slice(c * FF_TILE, (c + 1) * FF_TILE)
        u = jnp.dot(hn, w1_ref[:, sl], preferred_element_type=F32)
        a = jnp.square(jnp.maximum(u, 0.0)).astype(BF16)
        acc = acc + jnp.dot(a, w2_ref[sl, :], preferred_element_type=F32)
    if final:
        acc = _rms(acc, gf_ref[...])
    out_ref[...] = acc


def _oproj_mlp(h, o, wo, g, w1_all, w2_all, layer, gf, final):
    t = h.shape[0]
    tm = MLP_ROW_TILE
    layer_spec = lambda a: pl.BlockSpec((None,) + a.shape[1:], lambda i: (layer, 0, 0),
                                        pipeline_mode=pl.Buffered(1))
    return pl.pallas_call(
        functools.partial(_oproj_mlp_kernel, final=final),
        grid=(t // tm,),
        in_specs=[
            pl.BlockSpec((tm, D_MODEL), lambda i: (i, 0)),
            pl.BlockSpec((tm, o.shape[1]), lambda i: (i, 0)),
            _const_spec(wo.shape),
            _const_spec((1, D_MODEL)),
            layer_spec(w1_all),
            layer_spec(w2_all),
            _const_spec((1, D_MODEL)),
        ],
        out_specs=pl.BlockSpec((tm, D_MODEL), lambda i: (i, 0)),
        out_shape=jax.ShapeDtypeStruct((t, D_MODEL), F32),
        compiler_params=pltpu.CompilerParams(
            dimension_semantics=("arbitrary",), vmem_limit_bytes=VMEM_LIMIT_BYTES),
        name="oproj_mlp_final" if final else "oproj_mlp",
    )(h, o, wo, g, w1_all, w2_all, gf)


def _gla_in_kernel(x_ref, g_ref, w_ref, q_ref, k_ref, v_ref, r_ref, z_ref):
    tm = x_ref.shape[0]
    v0 = 2 * GLA_QK
    r0 = v0 + GLA_V
    z0 = r0 + GLA_V
    xn = _rms(x_ref[...], g_ref[...]).astype(BF16)
    qk = jnp.dot(xn, w_ref[:, :v0], preferred_element_type=F32)
    q_ref[...] = (qk[:, :GLA_QK] * (GLA_DK ** -0.5)).astype(BF16)
    k_ref[...] = qk[:, GLA_QK:].astype(BF16)
    v_ref[...] = jnp.dot(xn, w_ref[:, v0:r0], preferred_element_type=F32).astype(BF16)
    r_ref[...] = jnp.dot(xn, w_ref[:, r0:z0], preferred_element_type=F32).astype(BF16)
    codes = jnp.dot(xn, w_ref[:, z0:], preferred_element_type=F32)
    z_ref[...] = jnp.concatenate(
        [codes, jnp.zeros((tm, LANES - 2 * GATE_RANK), F32)], axis=1)


def _gla_inproj(h, g, w_in):
    t = h.shape[0]
    tm = ROW_TILE
    row = lambda i: (i, 0)
    return pl.pallas_call(
        _gla_in_kernel,
        grid=(t // tm,),
        in_specs=[
            pl.BlockSpec((tm, D_MODEL), row),
            _const_spec((1, D_MODEL)),
            _const_spec(w_in.shape),
        ],
        out_specs=[
            pl.BlockSpec((tm, GLA_QK), row),
            pl.BlockSpec((tm, GLA_QK), row),
            pl.BlockSpec((tm, GLA_V), row),
            pl.BlockSpec((tm, GLA_V), row),
            pl.BlockSpec((tm, LANES), row),
        ],
        out_shape=[
            jax.ShapeDtypeStruct((t, GLA_QK), BF16),
            jax.ShapeDtypeStruct((t, GLA_QK), BF16),
            jax.ShapeDtypeStruct((t, GLA_V), BF16),
            jax.ShapeDtypeStruct((t, GLA_V), BF16),
            jax.ShapeDtypeStruct((t, LANES), F32),
        ],
        compiler_params=pltpu.CompilerParams(
            dimension_semantics=("arbitrary",), vmem_limit_bytes=VMEM_LIMIT_BYTES),
        name="gla_inproj",
    )(h, g, w_in)


def _gla_kernel(q_ref, k_ref, v_ref, r_ref, z_ref, wcat_ref, bias_ref, og_ref, cum_ref, out_ref,
                lhs_scr, kv_scr, dec_scr, stage_scr):
    seq = q_ref.shape[0]
    c64 = GLA_CHUNK
    tile = GLA_TILE
    n_tiles = seq // tile
    grp = GLA_GROUP
    grp_rows = grp * tile
    dk = GLA_DK

    ridx = lax.broadcasted_iota(jnp.int32, (tile, tile), 0)
    cidx = lax.broadcasted_iota(jnp.int32, (tile, tile), 1)
    same_chunk = (ridx // c64) == (cidx // c64)
    fwd_diag = same_chunk & (ridx >= cidx)
    fwd_cross = (ridx >= c64) & (cidx < c64)
    bwd_diag = same_chunk & (ridx <= cidx)
    bwd_cross = (ridx < c64) & (cidx >= c64)
    second = lax.broadcasted_iota(jnp.int32, (tile, dk), 0) >= c64
    zeros_td = jnp.zeros((tile, dk), BF16)
    zeros_2td = jnp.zeros((2 * tile, dk), BF16)

    def gate_sums(g):
        goff = pl.multiple_of(g * grp_rows, grp_rows)
        z_hi, z_lo = _hi_lo(z_ref[pl.ds(goff, grp_rows), :])
        zcat = jnp.concatenate([z_hi, z_lo, z_hi], axis=1)
        logit = jnp.dot(zcat, wcat_ref[...], preferred_element_type=F32) + bias_ref[...]
        lg2 = (jnp.minimum(logit, 0.0) - jnp.log(1.0 + jnp.exp(-jnp.abs(logit)))) * (LOG2E / GATE_TAU)

        sums = []
        for t in range(grp):
            lg_hi, lg_lo = _hi_lo(lg2[t * tile:(t + 1) * tile])
            rhs = jnp.concatenate([
                jnp.concatenate([lg_hi[:, :dk], zeros_td], axis=1),
                jnp.concatenate([lg_lo[:, :dk], zeros_td], axis=1),
                jnp.concatenate([zeros_td, lg_hi[:, dk:]], axis=1),
                jnp.concatenate([zeros_td, lg_lo[:, dk:]], axis=1)], axis=0)
            sums.append(jnp.dot(cum_ref[...], rhs, preferred_element_type=F32))
        return sums

    def tile_operands(g, sums):
        goff = pl.multiple_of(g * grp_rows, grp_rows)
        scores = []
        kvs = []
        for t in range(grp):
            rows = pl.ds(goff + t * tile, tile)
            qf = q_ref[rows, :].astype(F32)
            kf = k_ref[rows, :].astype(F32)
            vv = v_ref[rows, :]
            b_f = sums[t][:, :dk]
            b_b = sums[t][:, dk:]
            tf0, tf1 = b_f[c64 - 1:c64], b_f[tile - 1:tile]
            tb0, tb1 = b_b[0:1], b_b[c64:c64 + 1]
            qe_f32 = qf * jnp.exp2(b_f)
            qe_f = qe_f32.astype(BF16)
            ke_f = (kf * jnp.exp2(-b_f)).astype(BF16)
            kd_f32 = kf * jnp.exp2(jnp.where(second, tf1, tf0) - b_f)
            qe_b32 = qf * jnp.exp2(b_b)
            qe_b = qe_b32.astype(BF16)
            ke_b = (kf * jnp.exp2(-b_b)).astype(BF16)
            kd_b32 = kf * jnp.exp2(jnp.where(second, tb1, tb0) - b_b)
            qt_f = (qe_f32 * jnp.where(second, jnp.exp2(tf0), 1.0)).astype(BF16)
            qt_b = (qe_b32 * jnp.where(second, 1.0, jnp.exp2(tb1))).astype(BF16)
            kt_f = kd_f32 * jnp.where(second, 1.0, jnp.exp2(tf1))
            kt_b = kd_b32 * jnp.where(second, jnp.exp2(tb0), 1.0)

            keys = jnp.concatenate([
                jnp.concatenate([ke_f, kd_f32.astype(BF16)], axis=0),
                jnp.concatenate([ke_b, kd_b32.astype(BF16)], axis=0)], axis=0)
            keys = jnp.concatenate([
                jnp.concatenate([keys[:2 * tile], zeros_2td], axis=1),
                jnp.concatenate([zeros_2td, keys[2 * tile:]], axis=1)], axis=0)
            sc = lax.dot_general(jnp.concatenate([qe_f, qe_b], axis=1), keys,
                                 (((1,), (1,)), ((), ())), preferred_element_type=F32)
            scores.append((sc, qt_f, qt_b))

            kt_both = jnp.concatenate([
                jnp.concatenate([kt_f.T.astype(BF16), zeros_td], axis=1),
                jnp.concatenate([zeros_td, kt_b.T.astype(BF16)], axis=1)], axis=0)
            kvs.append(jnp.dot(kt_both, jnp.concatenate([vv, vv], axis=0),
                               preferred_element_type=F32))
            idx = g * grp + t
            dec_scr[0, idx] = jnp.broadcast_to(jnp.exp2(tf0 + tf1), (dk, dk)).T
            dec_scr[1, idx] = jnp.broadcast_to(jnp.exp2(tb0 + tb1), (dk, dk)).T

        for t in range(grp):
            rows = pl.ds(goff + t * tile, tile)
            sc, qt_f, qt_b = scores[t]
            a_sum = (jnp.where(fwd_diag, sc[:, :tile], 0.0)
                     + jnp.where(fwd_cross, sc[:, tile:2 * tile], 0.0)
                     + jnp.where(bwd_diag, sc[:, 2 * tile:3 * tile], 0.0)
                     + jnp.where(bwd_cross, sc[:, 3 * tile:], 0.0))
            lhs_scr[rows, :] = jnp.concatenate([a_sum.astype(BF16), qt_f, qt_b], axis=1)
            idx = g * grp + t
            kv_scr[0, idx] = kvs[t][:dk]
            kv_scr[1, idx] = kvs[t][dk:]

    n_groups = n_tiles // grp

    def run_two_stage(first, second):
        def load():
            return [stage_scr[t * tile:(t + 1) * tile, :] for t in range(grp)]

        def store(vals):
            for t in range(grp):
                stage_scr[t * tile:(t + 1) * tile, :] = vals[t]

        store(first(0))

        def step(j, carry):
            g0 = 2 * j
            vals0 = load()
            vals1 = first(g0 + 1)
            second(g0, vals0)
            vals2 = first(jnp.minimum(g0 + 2, n_groups - 1))
            second(g0 + 1, vals1)
            store(vals2)
            return carry

        lax.fori_loop(0, n_groups // 2, step, 0)

    run_two_stage(gate_sums, tile_operands)

    for d in range(2):
        def step(i, run, d=d):
            t = i if d == 0 else n_tiles - 1 - i
            inc = kv_scr[d, t]
            kv_scr[d, t] = run
            dec = dec_scr[d, t]
            return jnp.concatenate([dec, dec], axis=1) * run + inc

        lax.fori_loop(0, n_tiles, step, jnp.zeros((dk, GLA_DV), F32))

    def out_dots(g):
        goff = pl.multiple_of(g * grp_rows, grp_rows)
        outs = []
        for t in range(grp):
            rows = pl.ds(goff + t * tile, tile)
            idx = g * grp + t
            rhs = jnp.concatenate([v_ref[rows, :], kv_scr[0, idx].astype(BF16),
                                   kv_scr[1, idx].astype(BF16)], axis=0)
            outs.append(jnp.dot(lhs_scr[rows, :], rhs, preferred_element_type=F32))
        return outs

    def norm_gate(g, outs):
        goff = pl.multiple_of(g * grp_rows, grp_rows)
        for t in range(grp):
            rows = pl.ds(goff + t * tile, tile)
            on = _rms(outs[t], og_ref[...])
            rr = r_ref[rows, :].astype(F32)
            out_ref[rows, :] = (on * (rr * jax.nn.sigmoid(rr))).astype(BF16)

    run_two_stage(out_dots, norm_gate)


def _gla_scan(q, k, v, r, z, wcat, bias, og, cum, batch, seq):
    t = q.shape[0]
    n_tiles = seq // GLA_TILE
    bh = lambda b, h: (b, h)
    return pl.pallas_call(
        _gla_kernel,
        grid=(batch, GLA_HEADS),
        in_specs=[
            pl.BlockSpec((seq, GLA_DK), bh),
            pl.BlockSpec((seq, GLA_DK), bh),
            pl.BlockSpec((seq, GLA_DV), bh),
            pl.BlockSpec((seq, GLA_DV), bh),
            pl.BlockSpec((seq, LANES), lambda b, h: (b, 0)),
            pl.BlockSpec((3 * LANES, 2 * GLA_DK), lambda b, h: (0, h)),
            pl.BlockSpec((1, 2 * GLA_DK), lambda b, h: (0, h)),
            _const_spec((1, GLA_DV)),
            _const_spec(cum.shape),
        ],
        out_specs=pl.BlockSpec((seq, GLA_DV), bh),
        out_shape=jax.ShapeDtypeStruct((t, GLA_V), BF16),
        scratch_shapes=[
            pltpu.VMEM((seq, GLA_TILE + 2 * GLA_DK), BF16),
            pltpu.VMEM((2, n_tiles, GLA_DK, GLA_DV), F32),
            pltpu.VMEM((2, n_tiles, GLA_DK, LANES), F32),
            pltpu.VMEM((GLA_GROUP * GLA_TILE, 2 * GLA_DK), F32),
        ],
        compiler_params=pltpu.CompilerParams(
            dimension_semantics=("arbitrary", "arbitrary"),
            vmem_limit_bytes=VMEM_LIMIT_BYTES),
        name="gla_scan",
    )(q, k, v, r, z, wcat, bias, og, cum)


def _rope_tables(seq):
    pos = jnp.arange(seq, dtype=jnp.int32)
    half = HEAD_DIM // 2
    inv = ROPE_THETA ** (-jnp.arange(0, half, 2, dtype=F32) / half)
    ang_r = (pos // GRID_W).astype(F32)[:, None] * inv[None, :]
    ang_c = (pos % GRID_W).astype(F32)[:, None] * inv[None, :]
    cos_h = jnp.concatenate([jnp.cos(ang_r)] * 2 + [jnp.cos(ang_c)] * 2, axis=-1)
    sin_h = jnp.concatenate([-jnp.sin(ang_r), jnp.sin(ang_r), -jnp.sin(ang_c), jnp.sin(ang_c)], axis=-1)
    return jnp.tile(cos_h, (1, 2)), jnp.tile(sin_h, (1, 2))


def _gla_cumsum_matrix():
    n = GLA_TILE
    r = jnp.arange(n)[:, None]
    c = jnp.arange(n)[None, :]
    same = (r // GLA_CHUNK) == (c // GLA_CHUNK)
    prefix = (same & (c <= r)).astype(BF16)
    suffix = (same & (c >= r)).astype(BF16)
    return jnp.concatenate([prefix, prefix, suffix, suffix], axis=1)


def kernel(x, norm_mix, norm_mlp, attn_w_qkv, attn_q_norm, attn_k_norm, attn_w_o,
           gla_w_in, gla_w_gate_up, gla_b_gate, gla_out_norm, gla_w_o,
           mlp_w_in, mlp_w_out, final_norm):
    batch, seq, d = x.shape
    t = batch * seq
    h = x.reshape(t, d)
    row = lambda a: a.reshape(1, -1).astype(F32)

    cos, sin = _rope_tables(seq)
    gidx = jnp.arange(2 * LANES)
    same_head = gidx[:, None] // HEAD_DIM == gidx[None, :] // HEAD_DIM
    gsum = jnp.where(same_head, 1.0 / HEAD_DIM, 0.0).astype(BF16)
    q_gain = jnp.tile(attn_q_norm[0], 2).reshape(1, LANES) * (LOG2E / math.sqrt(HEAD_DIM))
    k_gain = jnp.tile(attn_k_norm[0], 2).reshape(1, LANES)
    q, kt, v = _qkv_proj(h, row(norm_mix[0]), attn_w_qkv[0].astype(BF16), q_gain, k_gain,
                         cos, sin, gsum, batch, seq)
    o = _attention(q, kt, v, batch, seq)
    w1_all = mlp_w_in.astype(BF16)
    w2_all = mlp_w_out.astype(BF16)
    h = _oproj_mlp(h, o, attn_w_o[0].astype(BF16), row(norm_mlp[0]),
                   w1_all, w2_all, 0, row(final_norm), False)

    gq, gk, gv, gr, gz = _gla_inproj(h, row(norm_mix[1]), gla_w_in[0].astype(BF16))
    wup = jnp.zeros((2, LANES, GLA_QK), F32)
    wup = wup.at[0, :GATE_RANK].set(gla_w_gate_up[0, 0])
    wup = wup.at[1, GATE_RANK:2 * GATE_RANK].set(gla_w_gate_up[0, 1])
    wup = wup.reshape(2, LANES, GLA_HEADS, GLA_DK).transpose(1, 2, 0, 3).reshape(LANES, 2 * GLA_QK)
    wup_hi, wup_lo = _hi_lo(wup)
    wcat = jnp.concatenate([wup_hi, wup_hi, wup_lo], axis=0)
    b_gate = gla_b_gate[0].reshape(2, GLA_HEADS, GLA_DK).transpose(1, 0, 2).reshape(1, 2 * GLA_QK)
    go = _gla_scan(gq, gk, gv, gr, gz, wcat, b_gate,
                   row(gla_out_norm[0]), _gla_cumsum_matrix(), batch, seq)
    h = _oproj_mlp(h, go, gla_w_o[0].astype(BF16), row(norm_mlp[1]),
                   w1_all, w2_all, 1, row(final_norm), True)
    return h.reshape(batch, seq, d)
```

```python
import functools
import math

import jax
import jax.numpy as jnp
import numpy as np
from jax import lax
from jax.experimental import pallas as pl
from jax.experimental.pallas import tpu as pltpu

F32 = jnp.float32
BF16 = jnp.bfloat16

D_MODEL = 1024
GRID_W = 64
N_HEADS = 16
N_KV_HEADS = 4
HEAD_DIM = 64
GROUP = N_HEADS // N_KV_HEADS
ROPE_THETA = 10000.0
GLA_HEADS = 4
GLA_DK = 128
GLA_DV = 256
GLA_QK = GLA_HEADS * GLA_DK
GLA_V = GLA_HEADS * GLA_DV
GATE_RANK = 16
GATE_TAU = 16.0
D_FF = 4 * D_MODEL
EPS = 1e-6

LANES = 128
BF16_SUBLANES = 16
VMEM_LIMIT_BYTES = 56 * 1024 * 1024

ROW_TILE = 1024
MLP_ROW_TILE = 1024
QKV_SPLIT = 8
ATTN_TQ = 512
ATTN_TK = 512
FF_TILE = 1024
GLA_CHUNK = 64
GLA_TILE = 2 * GLA_CHUNK
GLA_GROUP = 4
LOG2E = 1.4426950408889634


def _const_spec(shape):
    nd = len(shape)
    return pl.BlockSpec(shape, lambda *_: (0,) * nd, pipeline_mode=pl.Buffered(1))


def _rms(x, gain):
    ms = jnp.mean(x * x, axis=-1, keepdims=True)
    return x * lax.rsqrt(ms + EPS) * gain


def _hi_lo(x_f32):
    hi = x_f32.astype(BF16)
    lo = (x_f32 - hi.astype(F32)).astype(BF16)
    return hi, lo


def _qkv_kernel(x_ref, g_ref, w_ref, qg_ref, kg_ref, cos_ref, sin_ref, gsum_ref,
                q_ref, kt_ref, v_ref):
    tm = x_ref.shape[0]
    sub = tm // QKV_SPLIT
    lane = lax.broadcasted_iota(jnp.int32, (sub, LANES), 1)
    first_half = (lane % 32) < 16
    gsum = gsum_ref[...]
    nq = N_HEADS * HEAD_DIM
    nk = N_KV_HEADS * HEAD_DIM

    for r in range(QKV_SPLIT):
        rows = slice(r * sub, (r + 1) * sub)
        xn = _rms(x_ref[rows, :], g_ref[...]).astype(BF16)
        cos = cos_ref[rows, :]
        sin = sin_ref[rows, :]

        def norm_rope(c4, gain):
            ms = jnp.dot((c4 * c4).astype(BF16), gsum, preferred_element_type=F32)
            halves = []
            for hf in range(2):
                sl = slice(hf * LANES, (hf + 1) * LANES)
                cn = c4[:, sl] * lax.rsqrt(ms[:, sl] + EPS) * gain
                rot = jnp.where(first_half, pltpu.roll(cn, LANES - 16, 1), pltpu.roll(cn, 16, 1))
                halves.append(cn * cos + rot * sin)
            return halves

        y = jnp.dot(xn, w_ref[...], preferred_element_type=F32)
        for j in range(nq // (2 * LANES)):
            for hf, qh in enumerate(norm_rope(y[:, 2 * j * LANES:2 * (j + 1) * LANES], qg_ref[...])):
                col = (2 * j + hf) * LANES
                q_ref[rows, col:col + LANES] = qh.astype(BF16)
        for j, kh in enumerate(norm_rope(y[:, nq:nq + nk], kg_ref[...])):
            kt = kh.T
            kt_ref[0, 2 * j, :, rows] = kt[:HEAD_DIM].astype(BF16)
            kt_ref[0, 2 * j + 1, :, rows] = kt[HEAD_DIM:].astype(BF16)
        for g in range(N_KV_HEADS):
            c0 = nq + nk + (g // 2) * LANES
            pair = y[:, c0:c0 + LANES]
            if g % 2:
                pair = pltpu.roll(pair, HEAD_DIM, 1)
            v_ref[rows, g * LANES:(g + 1) * LANES] = jnp.where(lane < HEAD_DIM, pair, 1.0).astype(BF16)


def _qkv_proj(x2, g, w, qg, kg, cos, sin, gsum, batch, seq):
    t = x2.shape[0]
    tm = ROW_TILE
    spb = seq // tm
    nkv = N_KV_HEADS * LANES
    return pl.pallas_call(
        _qkv_kernel,
        grid=(t // tm,),
        in_specs=[
            pl.BlockSpec((tm, D_MODEL), lambda i: (i, 0)),
            _const_spec((1, D_MODEL)),
            _const_spec(w.shape),
            _const_spec((1, LANES)),
            _const_spec((1, LANES)),
            pl.BlockSpec((tm, LANES), lambda i: (i % spb, 0)),
            pl.BlockSpec((tm, LANES), lambda i: (i % spb, 0)),
            _const_spec(gsum.shape),
        ],
        out_specs=[
            pl.BlockSpec((tm, N_HEADS * HEAD_DIM), lambda i: (i, 0)),
            pl.BlockSpec((1, N_KV_HEADS, HEAD_DIM, tm), lambda i: (i // spb, 0, 0, i % spb)),
            pl.BlockSpec((tm, nkv), lambda i: (i, 0)),
        ],
        out_shape=[
            jax.ShapeDtypeStruct((t, N_HEADS * HEAD_DIM), BF16),
            jax.ShapeDtypeStruct((batch, N_KV_HEADS, HEAD_DIM, seq), BF16),
            jax.ShapeDtypeStruct((t, nkv), BF16),
        ],
        compiler_params=pltpu.CompilerParams(
            dimension_semantics=("arbitrary",), vmem_limit_bytes=VMEM_LIMIT_BYTES),
        name="qkv_proj",
    )(x2, g, w, qg, kg, cos, sin, gsum)


def _attn_kernel(q_ref, kt_ref, v_ref, *refs):
    n_cast = (len(refs) - 1) // 2
    cast_src, o_ref, cast_dst = refs[:n_cast], refs[n_cast], refs[n_cast + 1:]
    for src, dst in zip(cast_src, cast_dst):
        dst[...] = src[...].astype(BF16)

    tq = q_ref.shape[0]
    seq = v_ref.shape[0]
    q4 = jnp.concatenate(
        [q_ref[:, h * HEAD_DIM:(h + 1) * HEAD_DIM] for h in range(GROUP)], axis=0)
    m = None
    acc = None
    for c in range(seq // ATTN_TK):
        ks = slice(c * ATTN_TK, (c + 1) * ATTN_TK)
        s = jnp.dot(q4, kt_ref[0, 0, :, ks], preferred_element_type=F32)
        m_c = jnp.max(s, axis=-1, keepdims=True)
        m_new = m_c if m is None else jnp.maximum(m, m_c)
        p = jnp.exp2((s - m_new).astype(BF16))
        pv = jnp.dot(p, v_ref[ks, :], preferred_element_type=F32)
        acc = pv if acc is None else acc * jnp.exp2(m - m_new) + pv
        m = m_new
    out = acc * (1.0 / pltpu.roll(acc, HEAD_DIM, 1))
    lane = lax.broadcasted_iota(jnp.int32, (tq, LANES), 1)
    for hp in range(GROUP // 2):
        even = out[(2 * hp) * tq:(2 * hp + 1) * tq]
        odd = pltpu.roll(out[(2 * hp + 1) * tq:(2 * hp + 2) * tq], HEAD_DIM, 1)
        o_ref[:, hp * LANES:(hp + 1) * LANES] = jnp.where(lane < HEAD_DIM, even, odd).astype(BF16)


def _attention(q, kt, v, batch, seq, cast_along):
    t = q.shape[0]
    tq = ATTN_TQ
    nq = seq // tq
    n_steps = batch * N_KV_HEADS * nq
    grp_w = GROUP * HEAD_DIM

    def cast_spec(a):
        rows = max(BF16_SUBLANES, a.shape[0] // n_steps)
        last = a.shape[0] // rows - 1
        return pl.BlockSpec(
            (rows, a.shape[1]),
            lambda b, g, i: (jnp.minimum((b * N_KV_HEADS + g) * nq + i, last), 0))

    cast_specs = [cast_spec(a) for a in cast_along]
    outs = pl.pallas_call(
        _attn_kernel,
        grid=(batch, N_KV_HEADS, nq),
        in_specs=[
            pl.BlockSpec((tq, grp_w), lambda b, g, i: (b * nq + i, g)),
            pl.BlockSpec((1, 1, HEAD_DIM, seq), lambda b, g, i: (b, g, 0, 0)),
            pl.BlockSpec((seq, LANES), lambda b, g, i: (b, g)),
        ] + cast_specs,
        out_specs=[pl.BlockSpec((tq, grp_w), lambda b, g, i: (b * nq + i, g))] + cast_specs,
        out_shape=[jax.ShapeDtypeStruct((t, N_HEADS * HEAD_DIM), BF16)]
        + [jax.ShapeDtypeStruct(a.shape, BF16) for a in cast_along],
        compiler_params=pltpu.CompilerParams(
            dimension_semantics=("arbitrary", "arbitrary", "arbitrary"),
            vmem_limit_bytes=VMEM_LIMIT_BYTES),
        name="attention",
    )(q, kt, v, *cast_along)
    return outs[0], outs[1:]


def _oproj_mlp_kernel(h_ref, o_ref, wo_ref, g_ref, w1_ref, w2_ref, gf_ref, out_ref, *, final):
    h1 = h_ref[...] + jnp.dot(o_ref[...], wo_ref[...], preferred_element_type=F32)
    hn = _rms(h1, g_ref[...]).astype(BF16)
    acc = h1
    for c in range(D_FF // FF_TILE):
        sl = slice(c * FF_TILE, (c + 1) * FF_TILE)
        u = jnp.dot(hn, w1_ref[:, sl], preferred_element_type=F32)
        a = jnp.square(jnp.maximum(u, 0.0)).astype(BF16)
        acc = acc + jnp.dot(a, w2_ref[sl, :], preferred_element_type=F32)
    if final:
        acc = _rms(acc, gf_ref[...])
    out_ref[...] = acc


def _oproj_mlp(h, o, wo, g, w1_all, w2_all, layer, gf, final):
    t = h.shape[0]
    tm = MLP_ROW_TILE
    layer_spec = lambda a: pl.BlockSpec((None,) + a.shape[1:], lambda i: (layer, 0, 0),
                                        pipeline_mode=pl.Buffered(1))
    return pl.pallas_call(
        functools.partial(_oproj_mlp_kernel, final=final),
        grid=(t // tm,),
        in_specs=[
            pl.BlockSpec((tm, D_MODEL), lambda i: (i, 0)),
            pl.BlockSpec((tm, o.shape[1]), lambda i: (i, 0)),
            _const_spec(wo.shape),
            _const_spec((1, D_MODEL)),
            layer_spec(w1_all),
            layer_spec(w2_all),
            _const_spec((1, D_MODEL)),
        ],
        out_specs=pl.BlockSpec((tm, D_MODEL), lambda i: (i, 0)),
        out_shape=jax.ShapeDtypeStruct((t, D_MODEL), F32),
        compiler_params=pltpu.CompilerParams(
            dimension_semantics=("arbitrary",), vmem_limit_bytes=VMEM_LIMIT_BYTES),
        name="oproj_mlp_final" if final else "oproj_mlp",
    )(h, o, wo, g, w1_all, w2_all, gf)


def _gla_in_kernel(x_ref, g_ref, w_ref, q_ref, k_ref, v_ref, r_ref, z_ref):
    tm = x_ref.shape[0]
    v0 = 2 * GLA_QK
    r0 = v0 + GLA_V
    z0 = r0 + GLA_V
    xn = _rms(x_ref[...], g_ref[...]).astype(BF16)
    qk = jnp.dot(xn, w_ref[:, :v0], preferred_element_type=F32)
    q_ref[...] = (qk[:, :GLA_QK] * (GLA_DK ** -0.5)).astype(BF16)
    k_ref[...] = qk[:, GLA_QK:].astype(BF16)
    v_ref[...] = jnp.dot(xn, w_ref[:, v0:r0], preferred_element_type=F32).astype(BF16)
    r_ref[...] = jnp.dot(xn, w_ref[:, r0:z0], preferred_element_type=F32).astype(BF16)
    codes = jnp.dot(xn, w_ref[:, z0:], preferred_element_type=F32)
    z_ref[...] = jnp.concatenate(
        [codes, jnp.zeros((tm, LANES - 2 * GATE_RANK), F32)], axis=1)


def _gla_inproj(h, g, w_in):
    t = h.shape[0]
    tm = ROW_TILE
    row = lambda i: (i, 0)
    return pl.pallas_call(
        _gla_in_kernel,
        grid=(t // tm,),
        in_specs=[
            pl.BlockSpec((tm, D_MODEL), row),
            _const_spec((1, D_MODEL)),
            _const_spec(w_in.shape),
        ],
        out_specs=[
            pl.BlockSpec((tm, GLA_QK), row),
            pl.BlockSpec((tm, GLA_QK), row),
            pl.BlockSpec((tm, GLA_V), row),
            pl.BlockSpec((tm, GLA_V), row),
            pl.BlockSpec((tm, LANES), row),
        ],
        out_shape=[
            jax.ShapeDtypeStruct((t, GLA_QK), BF16),
            jax.ShapeDtypeStruct((t, GLA_QK), BF16),
            jax.ShapeDtypeStruct((t, GLA_V), BF16),
            jax.ShapeDtypeStruct((t, GLA_V), BF16),
            jax.ShapeDtypeStruct((t, LANES), F32),
        ],
        compiler_params=pltpu.CompilerParams(
            dimension_semantics=("arbitrary",), vmem_limit_bytes=VMEM_LIMIT_BYTES),
        name="gla_inproj",
    )(h, g, w_in)


def _gla_kernel(q_ref, k_ref, v_ref, r_ref, z_ref, wcat_ref, bias_ref, og_ref, cum_ref, out_ref,
                lhs_scr, kv_scr, dec_scr, stage_scr):
    seq = q_ref.shape[0]
    c64 = GLA_CHUNK
    tile = GLA_TILE
    n_tiles = seq // tile
    grp = GLA_GROUP
    grp_rows = grp * tile
    dk = GLA_DK

    ridx = lax.broadcasted_iota(jnp.int32, (tile, tile), 0)
    cidx = lax.broadcasted_iota(jnp.int32, (tile, tile), 1)
    same_chunk = (ridx // c64) == (cidx // c64)
    fwd_diag = same_chunk & (ridx >= cidx)
    fwd_cross = (ridx >= c64) & (cidx < c64)
    bwd_diag = same_chunk & (ridx <= cidx)
    bwd_cross = (ridx < c64) & (cidx >= c64)
    second = lax.broadcasted_iota(jnp.int32, (tile, dk), 0) >= c64
    zeros_td = jnp.zeros((tile, dk), BF16)
    zeros_2td = jnp.zeros((2 * tile, dk), BF16)

    def gate_sums(g):
        goff = pl.multiple_of(g * grp_rows, grp_rows)
        z_hi, z_lo = _hi_lo(z_ref[pl.ds(goff, grp_rows), :])
        zcat = jnp.concatenate([z_hi, z_lo, z_hi], axis=1)
        logit = jnp.dot(zcat, wcat_ref[...], preferred_element_type=F32) + bias_ref[...]
        lg2 = (jnp.minimum(logit, 0.0) - jnp.log(1.0 + jnp.exp(-jnp.abs(logit)))) * (LOG2E / GATE_TAU)

        sums = []
        for t in range(grp):
            lg_hi, lg_lo = _hi_lo(lg2[t * tile:(t + 1) * tile])
            rhs = jnp.concatenate([
                jnp.concatenate([lg_hi[:, :dk], zeros_td], axis=1),
                jnp.concatenate([lg_lo[:, :dk], zeros_td], axis=1),
                jnp.concatenate([zeros_td, lg_hi[:, dk:]], axis=1),
                jnp.concatenate([zeros_td, lg_lo[:, dk:]], axis=1)], axis=0)
            sums.append(jnp.dot(cum_ref[...], rhs, preferred_element_type=F32))
        return sums

    def tile_operands(g, sums):
        goff = pl.multiple_of(g * grp_rows, grp_rows)
        scores = []
        kvs = []
        for t in range(grp):
            rows = pl.ds(goff + t * tile, tile)
            qf = q_ref[rows, :].astype(F32)
            kf = k_ref[rows, :].astype(F32)
            vv = v_ref[rows, :]
            b_f = sums[t][:, :dk]
            b_b = sums[t][:, dk:]
            tf0, tf1 = b_f[c64 - 1:c64], b_f[tile - 1:tile]
            tb0, tb1 = b_b[0:1], b_b[c64:c64 + 1]
            qe_f32 = qf * jnp.exp2(b_f)
            qe_f = qe_f32.astype(BF16)
            ke_f = (kf * jnp.exp2(-b_f)).astype(BF16)
            kd_f32 = kf * jnp.exp2(jnp.where(second, tf1, tf0) - b_f)
            qe_b32 = qf * jnp.exp2(b_b)
            qe_b = qe_b32.astype(BF16)
            ke_b = (kf * jnp.exp2(-b_b)).astype(BF16)
            kd_b32 = kf * jnp.exp2(jnp.where(second, tb1, tb0) - b_b)
            qt_f = (qe_f32 * jnp.where(second, jnp.exp2(tf0), 1.0)).astype(BF16)
            qt_b = (qe_b32 * jnp.where(second, 1.0, jnp.exp2(tb1))).astype(BF16)
            kt_f = kd_f32 * jnp.where(second, 1.0, jnp.exp2(tf1))
            kt_b = kd_b32 * jnp.where(second, jnp.exp2(tb0), 1.0)

            keys = jnp.concatenate([
                jnp.concatenate([ke_f, kd_f32.astype(BF16)], axis=0),
                jnp.concatenate([ke_b, kd_b32.astype(BF16)], axis=0)], axis=0)
            keys = jnp.concatenate([
                jnp.concatenate([keys[:2 * tile], zeros_2td], axis=1),
                jnp.concatenate([zeros_2td, keys[2 * tile:]], axis=1)], axis=0)
            sc = lax.dot_general(jnp.concatenate([qe_f, qe_b], axis=1), keys,
                                 (((1,), (1,)), ((), ())), preferred_element_type=F32)
            scores.append((sc, qt_f, qt_b))

            kt_both = jnp.concatenate([
                jnp.concatenate([kt_f.T.astype(BF16), zeros_td], axis=1),
                jnp.concatenate([zeros_td, kt_b.T.astype(BF16)], axis=1)], axis=0)
            kvs.append(jnp.dot(kt_both, jnp.concatenate([vv, vv], axis=0),
                               preferred_element_type=F32))
            idx = g * grp + t
            dec_scr[0, idx] = jnp.broadcast_to(jnp.exp2(tf0 + tf1), (dk, dk)).T
            dec_scr[1, idx] = jnp.broadcast_to(jnp.exp2(tb0 + tb1), (dk, dk)).T

        for t in range(grp):
            rows = pl.ds(goff + t * tile, tile)
            sc, qt_f, qt_b = scores[t]
            a_sum = (jnp.where(fwd_diag, sc[:, :tile], 0.0)
                     + jnp.where(fwd_cross, sc[:, tile:2 * tile], 0.0)
                     + jnp.where(bwd_diag, sc[:, 2 * tile:3 * tile], 0.0)
                     + jnp.where(bwd_cross, sc[:, 3 * tile:], 0.0))
            lhs_scr[rows, :] = jnp.concatenate([a_sum.astype(BF16), qt_f, qt_b], axis=1)
            idx = g * grp + t
            kv_scr[0, idx] = kvs[t][:dk]
            kv_scr[1, idx] = kvs[t][dk:]

    n_groups = n_tiles // grp

    def run_two_stage(first, second):
        def load():
            return [stage_scr[t * tile:(t + 1) * tile, :] for t in range(grp)]

        def store(vals):
            for t in range(grp):
                stage_scr[t * tile:(t + 1) * tile, :] = vals[t]

        store(first(0))

        def step(j, carry):
            g0 = 2 * j
            vals0 = load()
            vals1 = first(g0 + 1)
            second(g0, vals0)
            vals2 = first(jnp.minimum(g0 + 2, n_groups - 1))
            second(g0 + 1, vals1)
            store(vals2)
            return carry

        lax.fori_loop(0, n_groups // 2, step, 0)

    run_two_stage(gate_sums, tile_operands)

    for d in range(2):
        def step(i, run, d=d):
            t = i if d == 0 else n_tiles - 1 - i
            inc = kv_scr[d, t]
            kv_scr[d, t] = run
            dec = dec_scr[d, t]
            return jnp.concatenate([dec, dec], axis=1) * run + inc

        lax.fori_loop(0, n_tiles, step, jnp.zeros((dk, GLA_DV), F32))

    def out_dots(g):
        goff = pl.multiple_of(g * grp_rows, grp_rows)
        outs = []
        for t in range(grp):
            rows = pl.ds(goff + t * tile, tile)
            idx = g * grp + t
            rhs = jnp.concatenate([v_ref[rows, :], kv_scr[0, idx].astype(BF16),
                                   kv_scr[1, idx].astype(BF16)], axis=0)
            outs.append(jnp.dot(lhs_scr[rows, :], rhs, preferred_element_type=F32))
        return outs

    def norm_gate(g, outs):
        goff = pl.multiple_of(g * grp_rows, grp_rows)
        for t in range(grp):
            rows = pl.ds(goff + t * tile, tile)
            on = _rms(outs[t], og_ref[...])
            rr = r_ref[rows, :].astype(F32)
            out_ref[rows, :] = (on * (rr * jax.nn.sigmoid(rr))).astype(BF16)

    run_two_stage(out_dots, norm_gate)


def _gla_scan(q, k, v, r, z, wcat, bias, og, cum, batch, seq):
    t = q.shape[0]
    n_tiles = seq // GLA_TILE
    bh = lambda b, h: (b, h)
    return pl.pallas_call(
        _gla_kernel,
        grid=(batch, GLA_HEADS),
        in_specs=[
            pl.BlockSpec((seq, GLA_DK), bh),
            pl.BlockSpec((seq, GLA_DK), bh),
            pl.BlockSpec((seq, GLA_DV), bh),
            pl.BlockSpec((seq, GLA_DV), bh),
            pl.BlockSpec((seq, LANES), lambda b, h: (b, 0)),
            pl.BlockSpec((3 * LANES, 2 * GLA_DK), lambda b, h: (0, h)),
            pl.BlockSpec((1, 2 * GLA_DK), lambda b, h: (0, h)),
            _const_spec((1, GLA_DV)),
            _const_spec(cum.shape),
        ],
        out_specs=pl.BlockSpec((seq, GLA_DV), bh),
        out_shape=jax.ShapeDtypeStruct((t, GLA_V), BF16),
        scratch_shapes=[
            pltpu.VMEM((seq, GLA_TILE + 2 * GLA_DK), BF16),
            pltpu.VMEM((2, n_tiles, GLA_DK, GLA_DV), F32),
            pltpu.VMEM((2, n_tiles, GLA_DK, LANES), F32),
            pltpu.VMEM((GLA_GROUP * GLA_TILE, 2 * GLA_DK), F32),
        ],
        compiler_params=pltpu.CompilerParams(
            dimension_semantics=("arbitrary", "arbitrary"),
            vmem_limit_bytes=VMEM_LIMIT_BYTES),
        name="gla_scan",
    )(q, k, v, r, z, wcat, bias, og, cum)


def _rope_tables(seq):
    f32 = np.float32
    pos = np.arange(seq)
    half = HEAD_DIM // 2
    inv = (f32(ROPE_THETA) ** (-np.arange(0, half, 2, dtype=f32) / f32(half))).astype(f32)
    ang_r = (pos // GRID_W).astype(f32)[:, None] * inv[None, :]
    ang_c = (pos % GRID_W).astype(f32)[:, None] * inv[None, :]
    cos_h = np.concatenate([np.cos(ang_r)] * 2 + [np.cos(ang_c)] * 2, axis=-1)
    sin_h = np.concatenate([-np.sin(ang_r), np.sin(ang_r), -np.sin(ang_c), np.sin(ang_c)], axis=-1)
    return (jnp.asarray(np.tile(cos_h, (1, 2)), dtype=F32),
            jnp.asarray(np.tile(sin_h, (1, 2)), dtype=F32))


def _gla_cumsum_matrix():
    n = GLA_TILE
    r = np.arange(n)[:, None]
    c = np.arange(n)[None, :]
    same = (r // GLA_CHUNK) == (c // GLA_CHUNK)
    prefix = (same & (c <= r)).astype(np.float32)
    suffix = (same & (c >= r)).astype(np.float32)
    return jnp.asarray(np.concatenate([prefix, prefix, suffix, suffix], axis=1), dtype=BF16)


def kernel(x, norm_mix, norm_mlp, attn_w_qkv, attn_q_norm, attn_k_norm, attn_w_o,
           gla_w_in, gla_w_gate_up, gla_b_gate, gla_out_norm, gla_w_o,
           mlp_w_in, mlp_w_out, final_norm):
    batch, seq, d = x.shape
    t = batch * seq
    h = x.reshape(t, d)
    row = lambda a: a.reshape(1, -1).astype(F32)

    cos, sin = _rope_tables(seq)
    gidx = np.arange(2 * LANES)
    same_head = gidx[:, None] // HEAD_DIM == gidx[None, :] // HEAD_DIM
    gsum = jnp.asarray(np.where(same_head, 1.0 / HEAD_DIM, 0.0), dtype=BF16)
    q_gain = jnp.tile(attn_q_norm[0], 2).reshape(1, LANES) * (LOG2E / math.sqrt(HEAD_DIM))
    k_gain = jnp.tile(attn_k_norm[0], 2).reshape(1, LANES)
    q, kt, v = _qkv_proj(h, row(norm_mix[0]), attn_w_qkv[0].astype(BF16), q_gain, k_gain,
                         cos, sin, gsum, batch, seq)
    n_layers = mlp_w_in.shape[0]
    o, (w1_all, w2_all, w_attn_o, w_gla_in, w_gla_o) = _attention(
        q, kt, v, batch, seq,
        [mlp_w_in.reshape(n_layers * d, D_FF), mlp_w_out.reshape(n_layers * D_FF, d),
         attn_w_o[0], gla_w_in[0], gla_w_o[0]])
    w1_all = w1_all.reshape(n_layers, d, D_FF)
    w2_all = w2_all.reshape(n_layers, D_FF, d)
    h = _oproj_mlp(h, o, w_attn_o, row(norm_mlp[0]),
                   w1_all, w2_all, 0, row(final_norm), False)

    gq, gk, gv, gr, gz = _gla_inproj(h, row(norm_mix[1]), w_gla_in)
    wup = jnp.zeros((2, LANES, GLA_QK), F32)
    wup = wup.at[0, :GATE_RANK].set(gla_w_gate_up[0, 0])
    wup = wup.at[1, GATE_RANK:2 * GATE_RANK].set(gla_w_gate_up[0, 1])
    wup = wup.reshape(2, LANES, GLA_HEADS, GLA_DK).transpose(1, 2, 0, 3).reshape(LANES, 2 * GLA_QK)
    wup_hi, wup_lo = _hi_lo(wup)
    wcat = jnp.concatenate([wup_hi, wup_hi, wup_lo], axis=0)
    b_gate = gla_b_gate[0].reshape(2, GLA_HEADS, GLA_DK).transpose(1, 0, 2).reshape(1, 2 * GLA_QK)
    go = _gla_scan(gq, gk, gv, gr, gz, wcat, b_gate,
                   row(gla_out_norm[0]), _gla_cumsum_matrix(), batch, seq)
    h = _oproj_mlp(h, go, w_gla_o, row(norm_mlp[1]),
                   w1_all, w2_all, 1, row(final_norm), True)
    return h.reshape(batch, seq, d)
```

```python
import functools
import math

import jax
import jax.numpy as jnp
import numpy as np
from jax import lax
from jax.experimental import pallas as pl
from jax.experimental.pallas import tpu as pltpu

F32 = jnp.float32
BF16 = jnp.bfloat16

D_MODEL = 1024
GRID_W = 64
N_HEADS = 16
N_KV_HEADS = 4
HEAD_DIM = 64
GROUP = N_HEADS // N_KV_HEADS
ROPE_THETA = 10000.0
GLA_HEADS = 4
GLA_DK = 128
GLA_DV = 256
GLA_QK = GLA_HEADS * GLA_DK
GLA_V = GLA_HEADS * GLA_DV
GATE_RANK = 16
GATE_TAU = 16.0
D_FF = 4 * D_MODEL
EPS = 1e-6

LANES = 128
BF16_SUBLANES = 16
VMEM_LIMIT_BYTES = 56 * 1024 * 1024

ROW_TILE = 1024
MLP_ROW_TILE = 1024
QKV_SPLIT = 8
ATTN_TQ = 512
ATTN_SUB_TQ = 256
ATTN_TK = 512
FF_TILE = 1024
GLA_CHUNK = 64
GLA_TILE = 2 * GLA_CHUNK
GLA_GROUP = 4
LOG2E = 1.4426950408889634


def _const_spec(shape):
    nd = len(shape)
    return pl.BlockSpec(shape, lambda *_: (0,) * nd, pipeline_mode=pl.Buffered(1))


def _rms(x, gain):
    ms = jnp.mean(x * x, axis=-1, keepdims=True)
    return x * lax.rsqrt(ms + EPS) * gain


def _hi_lo(x_f32):
    hi = x_f32.astype(BF16)
    lo = (x_f32 - hi.astype(F32)).astype(BF16)
    return hi, lo


def _qkv_kernel(x_ref, g_ref, w_ref, qg_ref, kg_ref, cos_ref, sin_ref, gsum_ref,
                q_ref, kt_ref, v_ref):
    tm = x_ref.shape[0]
    sub = tm // QKV_SPLIT
    lane = lax.broadcasted_iota(jnp.int32, (sub, LANES), 1)
    first_half = (lane % 32) < 16
    gsum = gsum_ref[...]
    nq = N_HEADS * HEAD_DIM
    nk = N_KV_HEADS * HEAD_DIM

    for r in range(QKV_SPLIT):
        rows = slice(r * sub, (r + 1) * sub)
        xn = _rms(x_ref[rows, :], g_ref[...]).astype(BF16)
        cos = cos_ref[rows, :]
        sin = sin_ref[rows, :]

        def norm_rope(c4, gain):
            ms = jnp.dot((c4 * c4).astype(BF16), gsum, preferred_element_type=F32)
            halves = []
            for hf in range(2):
                sl = slice(hf * LANES, (hf + 1) * LANES)
                cn = c4[:, sl] * lax.rsqrt(ms[:, sl] + EPS) * gain
                rot = jnp.where(first_half, pltpu.roll(cn, LANES - 16, 1), pltpu.roll(cn, 16, 1))
                halves.append(cn * cos + rot * sin)
            return halves

        y = jnp.dot(xn, w_ref[...], preferred_element_type=F32)
        for j in range(nq // (2 * LANES)):
            for hf, qh in enumerate(norm_rope(y[:, 2 * j * LANES:2 * (j + 1) * LANES], qg_ref[...])):
                col = (2 * j + hf) * LANES
                q_ref[rows, col:col + LANES] = qh.astype(BF16)
        for j, kh in enumerate(norm_rope(y[:, nq:nq + nk], kg_ref[...])):
            kt = kh.T
            kt_ref[0, 2 * j, :, rows] = kt[:HEAD_DIM].astype(BF16)
            kt_ref[0, 2 * j + 1, :, rows] = kt[HEAD_DIM:].astype(BF16)
        for g in range(N_KV_HEADS):
            c0 = nq + nk + (g // 2) * LANES
            pair = y[:, c0:c0 + LANES]
            if g % 2:
                pair = pltpu.roll(pair, HEAD_DIM, 1)
            v_ref[rows, g * LANES:(g + 1) * LANES] = jnp.where(lane < HEAD_DIM, pair, 1.0).astype(BF16)


def _qkv_proj(x2, g, w, qg, kg, cos, sin, gsum, batch, seq):
    t = x2.shape[0]
    tm = ROW_TILE
    spb = seq // tm
    nkv = N_KV_HEADS * LANES
    return pl.pallas_call(
        _qkv_kernel,
        grid=(t // tm,),
        in_specs=[
            pl.BlockSpec((tm, D_MODEL), lambda i: (i, 0)),
            _const_spec((1, D_MODEL)),
            _const_spec(w.shape),
            _const_spec((1, LANES)),
            _const_spec((1, LANES)),
            pl.BlockSpec((tm, LANES), lambda i: (i % spb, 0)),
            pl.BlockSpec((tm, LANES), lambda i: (i % spb, 0)),
            _const_spec(gsum.shape),
        ],
        out_specs=[
            pl.BlockSpec((tm, N_HEADS * HEAD_DIM), lambda i: (i, 0)),
            pl.BlockSpec((1, N_KV_HEADS, HEAD_DIM, tm), lambda i: (i // spb, 0, 0, i % spb)),
            pl.BlockSpec((tm, nkv), lambda i: (i, 0)),
        ],
        out_shape=[
            jax.ShapeDtypeStruct((t, N_HEADS * HEAD_DIM), BF16),
            jax.ShapeDtypeStruct((batch, N_KV_HEADS, HEAD_DIM, seq), BF16),
            jax.ShapeDtypeStruct((t, nkv), BF16),
        ],
        compiler_params=pltpu.CompilerParams(
            dimension_semantics=("arbitrary",), vmem_limit_bytes=VMEM_LIMIT_BYTES),
        name="qkv_proj",
    )(x2, g, w, qg, kg, cos, sin, gsum)


def _attn_kernel(q_ref, kt_ref, v_ref, *refs):
    n_cast = (len(refs) - 1) // 2
    cast_src, o_ref, cast_dst = refs[:n_cast], refs[n_cast], refs[n_cast + 1:]
    for src, dst in zip(cast_src, cast_dst):
        dst[...] = src[...].astype(BF16)

    tq = ATTN_SUB_TQ
    seq = v_ref.shape[0]
    lane = lax.broadcasted_iota(jnp.int32, (tq, LANES), 1)
    for blk in range(q_ref.shape[0] // tq):
        rows = slice(blk * tq, (blk + 1) * tq)
        q4 = jnp.concatenate(
            [q_ref[rows, h * HEAD_DIM:(h + 1) * HEAD_DIM] for h in range(GROUP)], axis=0)
        m = None
        acc = None
        for c in range(seq // ATTN_TK):
            ks = slice(c * ATTN_TK, (c + 1) * ATTN_TK)
            s = jnp.dot(q4, kt_ref[0, 0, :, ks], preferred_element_type=F32)
            m_c = jnp.max(s, axis=-1, keepdims=True)
            m_new = m_c if m is None else jnp.maximum(m, m_c)
            p = jnp.exp2((s - m_new).astype(BF16))
            pv = jnp.dot(p, v_ref[ks, :], preferred_element_type=F32)
            acc = pv if acc is None else acc * jnp.exp2(m - m_new) + pv
            m = m_new
        out = acc * (1.0 / pltpu.roll(acc, HEAD_DIM, 1))
        for hp in range(GROUP // 2):
            even = out[(2 * hp) * tq:(2 * hp + 1) * tq]
            odd = pltpu.roll(out[(2 * hp + 1) * tq:(2 * hp + 2) * tq], HEAD_DIM, 1)
            o_ref[rows, hp * LANES:(hp + 1) * LANES] = jnp.where(lane < HEAD_DIM, even, odd).astype(BF16)


def _attention(q, kt, v, batch, seq, cast_along):
    t = q.shape[0]
    tq = ATTN_TQ
    nq = seq // tq
    n_steps = batch * N_KV_HEADS * nq
    grp_w = GROUP * HEAD_DIM

    def cast_spec(a):
        rows = max(BF16_SUBLANES, a.shape[0] // n_steps)
        last = a.shape[0] // rows - 1
        return pl.BlockSpec(
            (rows, a.shape[1]),
            lambda b, g, i: (jnp.minimum((b * N_KV_HEADS + g) * nq + i, last), 0))

    cast_specs = [cast_spec(a) for a in cast_along]
    outs = pl.pallas_call(
        _attn_kernel,
        grid=(batch, N_KV_HEADS, nq),
        in_specs=[
            pl.BlockSpec((tq, grp_w), lambda b, g, i: (b * nq + i, g)),
            pl.BlockSpec((1, 1, HEAD_DIM, seq), lambda b, g, i: (b, g, 0, 0)),
            pl.BlockSpec((seq, LANES), lambda b, g, i: (b, g)),
        ] + cast_specs,
        out_specs=[pl.BlockSpec((tq, grp_w), lambda b, g, i: (b * nq + i, g))] + cast_specs,
        out_shape=[jax.ShapeDtypeStruct((t, N_HEADS * HEAD_DIM), BF16)]
        + [jax.ShapeDtypeStruct(a.shape, BF16) for a in cast_along],
        compiler_params=pltpu.CompilerParams(
            dimension_semantics=("arbitrary", "arbitrary", "arbitrary"),
            vmem_limit_bytes=VMEM_LIMIT_BYTES),
        name="attention",
    )(q, kt, v, *cast_along)
    return outs[0], outs[1:]


def _oproj_mlp_kernel(h_ref, o_ref, wo_ref, g_ref, w1_ref, w2_ref, gf_ref, out_ref, *, final):
    h1 = h_ref[...] + jnp.dot(o_ref[...], wo_ref[...], preferred_element_type=F32)
    hn = _rms(h1, g_ref[...]).astype(BF16)
    acc = h1
    for c in range(D_FF // FF_TILE):
        sl = slice(c * FF_TILE, (c + 1) * FF_TILE)
        u = jnp.dot(hn, w1_ref[:, sl], preferred_element_type=F32)
        a = jnp.square(jnp.maximum(u, 0.0)).astype(BF16)
        acc = acc + jnp.dot(a, w2_ref[sl, :], preferred_element_type=F32)
    if final:
        acc = _rms(acc, gf_ref[...])
    out_ref[...] = acc


def _oproj_mlp(h, o, wo, g, w1_all, w2_all, layer, gf, final):
    t = h.shape[0]
    tm = MLP_ROW_TILE
    layer_spec = lambda a: pl.BlockSpec((None,) + a.shape[1:], lambda i: (layer, 0, 0),
                                        pipeline_mode=pl.Buffered(1))
    return pl.pallas_call(
        functools.partial(_oproj_mlp_kernel, final=final),
        grid=(t // tm,),
        in_specs=[
            pl.BlockSpec((tm, D_MODEL), lambda i: (i, 0)),
            pl.BlockSpec((tm, o.shape[1]), lambda i: (i, 0)),
            _const_spec(wo.shape),
            _const_spec((1, D_MODEL)),
            layer_spec(w1_all),
            layer_spec(w2_all),
            _const_spec((1, D_MODEL)),
        ],
        out_specs=pl.BlockSpec((tm, D_MODEL), lambda i: (i, 0)),
        out_shape=jax.ShapeDtypeStruct((t, D_MODEL), F32),
        compiler_params=pltpu.CompilerParams(
            dimension_semantics=("arbitrary",), vmem_limit_bytes=VMEM_LIMIT_BYTES),
        name="oproj_mlp_final" if final else "oproj_mlp",
    )(h, o, wo, g, w1_all, w2_all, gf)


def _gla_in_kernel(x_ref, g_ref, w_ref, wz_ref, wup_ref, bias_ref,
                   q_ref, k_ref, v_ref, r_ref, lg_hi_ref, lg_lo_ref):
    tm = x_ref.shape[0]
    v0 = 2 * GLA_QK
    r0 = v0 + GLA_V
    z0 = r0 + GLA_V
    xn = _rms(x_ref[...], g_ref[...]).astype(BF16)

    codes = jnp.dot(xn, wz_ref[...], preferred_element_type=F32)
    c_hi, c_lo = _hi_lo(codes)
    lane = lax.broadcasted_iota(jnp.int32, (tm, LANES), 1)
    in_lo_group = (lane >= 2 * GATE_RANK) & (lane < 4 * GATE_RANK)
    zcat = jnp.where(in_lo_group, c_lo, c_hi)

    def gate(head):
        cols = slice(head * 2 * GLA_DK, (head + 1) * 2 * GLA_DK)
        logit = jnp.dot(zcat, wup_ref[:, cols], preferred_element_type=F32) + bias_ref[:, cols]
        lg2 = (jnp.minimum(logit, 0.0) - jnp.log(1.0 + jnp.exp(-jnp.abs(logit)))) * (LOG2E / GATE_TAU)
        lg_hi, lg_lo = _hi_lo(lg2)
        lg_hi_ref[0, head] = lg_hi
        lg_lo_ref[0, head] = lg_lo

    def store_heads(ref, y, width):
        for hd in range(GLA_HEADS):
            ref[0, hd] = y[:, hd * width:(hd + 1) * width].astype(BF16)

    qk = jnp.dot(xn, w_ref[:, :v0], preferred_element_type=F32)
    store_heads(q_ref, qk[:, :GLA_QK] * (GLA_DK ** -0.5), GLA_DK)
    store_heads(k_ref, qk[:, GLA_QK:], GLA_DK)
    gate(0)
    store_heads(v_ref, jnp.dot(xn, w_ref[:, v0:r0], preferred_element_type=F32), GLA_DV)
    gate(1)
    gate(2)
    store_heads(r_ref, jnp.dot(xn, w_ref[:, r0:z0], preferred_element_type=F32), GLA_DV)
    gate(3)


def _gla_inproj(h, g, w_in, w_z, w_up, bias, batch, seq):
    t = h.shape[0]
    tm = ROW_TILE
    spb = seq // tm
    row = lambda i: (i, 0)

    def head_major(width):
        spec = pl.BlockSpec((1, GLA_HEADS, tm, width), lambda i: (i // spb, 0, i % spb, 0))
        return spec, jax.ShapeDtypeStruct((batch, GLA_HEADS, seq, width), BF16)

    outs = [head_major(GLA_DK), head_major(GLA_DK), head_major(GLA_DV), head_major(GLA_DV),
            head_major(2 * GLA_DK), head_major(2 * GLA_DK)]
    return pl.pallas_call(
        _gla_in_kernel,
        grid=(t // tm,),
        in_specs=[
            pl.BlockSpec((tm, D_MODEL), row),
            _const_spec((1, D_MODEL)),
            _const_spec(w_in.shape),
            _const_spec(w_z.shape),
            _const_spec(w_up.shape),
            _const_spec(bias.shape),
        ],
        out_specs=[spec for spec, _ in outs],
        out_shape=[shape for _, shape in outs],
        compiler_params=pltpu.CompilerParams(
            dimension_semantics=("arbitrary",), vmem_limit_bytes=VMEM_LIMIT_BYTES),
        name="gla_inproj",
    )(h, g, w_in, w_z, w_up, bias)


def _gla_kernel(q_ref, k_ref, v_ref, r_ref, lg_hi_ref, lg_lo_ref, og_ref, cum_ref, out_ref,
                lhs_scr, kv_scr, dec_scr, stage_scr):
    seq = q_ref.shape[0]
    c64 = GLA_CHUNK
    tile = GLA_TILE
    n_tiles = seq // tile
    grp = GLA_GROUP
    grp_rows = grp * tile
    dk = GLA_DK

    ridx = lax.broadcasted_iota(jnp.int32, (tile, tile), 0)
    cidx = lax.broadcasted_iota(jnp.int32, (tile, tile), 1)
    same_chunk = (ridx // c64) == (cidx // c64)
    fwd_diag = same_chunk & (ridx >= cidx)
    fwd_cross = (ridx >= c64) & (cidx < c64)
    bwd_diag = same_chunk & (ridx <= cidx)
    bwd_cross = (ridx < c64) & (cidx >= c64)
    second = lax.broadcasted_iota(jnp.int32, (tile, dk), 0) >= c64
    zeros_td = jnp.zeros((tile, dk), BF16)
    zeros_2td = jnp.zeros((2 * tile, dk), BF16)

    def gate_sums(g):
        goff = pl.multiple_of(g * grp_rows, grp_rows)
        sums = []
        for t in range(grp):
            rows = pl.ds(goff + t * tile, tile)
            lg_hi = lg_hi_ref[rows, :]
            lg_lo = lg_lo_ref[rows, :]
            rhs = jnp.concatenate([
                jnp.concatenate([lg_hi[:, :dk], zeros_td], axis=1),
                jnp.concatenate([lg_lo[:, :dk], zeros_td], axis=1),
                jnp.concatenate([zeros_td, lg_hi[:, dk:]], axis=1),
                jnp.concatenate([zeros_td, lg_lo[:, dk:]], axis=1)], axis=0)
            sums.append(jnp.dot(cum_ref[...], rhs, preferred_element_type=F32))
        return sums

    def tile_operands(g, sums):
        goff = pl.multiple_of(g * grp_rows, grp_rows)
        scores = []
        kvs = []
        for t in range(grp):
            rows = pl.ds(goff + t * tile, tile)
            qf = q_ref[rows, :].astype(F32)
            kf = k_ref[rows, :].astype(F32)
            vv = v_ref[rows, :]
            b_f = sums[t][:, :dk]
            b_b = sums[t][:, dk:]
            tf0, tf1 = b_f[c64 - 1:c64], b_f[tile - 1:tile]
            tb0, tb1 = b_b[0:1], b_b[c64:c64 + 1]
            qe_f32 = qf * jnp.exp2(b_f)
            qe_f = qe_f32.astype(BF16)
            ke_f = (kf * jnp.exp2(-b_f)).astype(BF16)
            kd_f32 = kf * jnp.exp2(jnp.where(second, tf1, tf0) - b_f)
            qe_b32 = qf * jnp.exp2(b_b)
            qe_b = qe_b32.astype(BF16)
            ke_b = (kf * jnp.exp2(-b_b)).astype(BF16)
            kd_b32 = kf * jnp.exp2(jnp.where(second, tb1, tb0) - b_b)
            qt_f = (qe_f32 * jnp.where(second, jnp.exp2(tf0), 1.0)).astype(BF16)
            qt_b = (qe_b32 * jnp.where(second, 1.0, jnp.exp2(tb1))).astype(BF16)
            kt_f = kd_f32 * jnp.where(second, 1.0, jnp.exp2(tf1))
            kt_b = kd_b32 * jnp.where(second, jnp.exp2(tb0), 1.0)

            keys = jnp.concatenate([
                jnp.concatenate([ke_f, kd_f32.astype(BF16)], axis=0),
                jnp.concatenate([ke_b, kd_b32.astype(BF16)], axis=0)], axis=0)
            keys = jnp.concatenate([
                jnp.concatenate([keys[:2 * tile], zeros_2td], axis=1),
                jnp.concatenate([zeros_2td, keys[2 * tile:]], axis=1)], axis=0)
            sc = lax.dot_general(jnp.concatenate([qe_f, qe_b], axis=1), keys,
                                 (((1,), (1,)), ((), ())), preferred_element_type=F32)
            scores.append((sc, qt_f, qt_b))

            kt_both = jnp.concatenate([
                jnp.concatenate([kt_f.T.astype(BF16), zeros_td], axis=1),
                jnp.concatenate([zeros_td, kt_b.T.astype(BF16)], axis=1)], axis=0)
            kvs.append(jnp.dot(kt_both, jnp.concatenate([vv, vv], axis=0),
                               preferred_element_type=F32))
            idx = g * grp + t
            dec_scr[0, idx] = jnp.broadcast_to(jnp.exp2(tf0 + tf1), (dk, dk)).T
            dec_scr[1, idx] = jnp.broadcast_to(jnp.exp2(tb0 + tb1), (dk, dk)).T

        for t in range(grp):
            rows = pl.ds(goff + t * tile, tile)
            sc, qt_f, qt_b = scores[t]
            a_sum = (jnp.where(fwd_diag, sc[:, :tile], 0.0)
                     + jnp.where(fwd_cross, sc[:, tile:2 * tile], 0.0)
                     + jnp.where(bwd_diag, sc[:, 2 * tile:3 * tile], 0.0)
                     + jnp.where(bwd_cross, sc[:, 3 * tile:], 0.0))
            lhs_scr[rows, :] = jnp.concatenate([a_sum.astype(BF16), qt_f, qt_b], axis=1)
            idx = g * grp + t
            kv_scr[0, idx] = kvs[t][:dk]
            kv_scr[1, idx] = kvs[t][dk:]

    n_groups = n_tiles // grp

    def run_two_stage(first, second):
        def load():
            return [stage_scr[t * tile:(t + 1) * tile, :] for t in range(grp)]

        def store(vals):
            for t in range(grp):
                stage_scr[t * tile:(t + 1) * tile, :] = vals[t]

        store(first(0))

        def step(j, carry):
            g0 = 2 * j
            vals0 = load()
            vals1 = first(g0 + 1)
            second(g0, vals0)
            vals2 = first(jnp.minimum(g0 + 2, n_groups - 1))
            second(g0 + 1, vals1)
            store(vals2)
            return carry

        lax.fori_loop(0, n_groups // 2, step, 0)

    run_two_stage(gate_sums, tile_operands)

    for d in range(2):
        def step(i, run, d=d):
            t = i if d == 0 else n_tiles - 1 - i
            inc = kv_scr[d, t]
            kv_scr[d, t] = run
            dec = dec_scr[d, t]
            return jnp.concatenate([dec, dec], axis=1) * run + inc

        lax.fori_loop(0, n_tiles, step, jnp.zeros((dk, GLA_DV), F32))

    def out_dots(g):
        goff = pl.multiple_of(g * grp_rows, grp_rows)
        outs = []
        for t in range(grp):
            rows = pl.ds(goff + t * tile, tile)
            idx = g * grp + t
            rhs = jnp.concatenate([v_ref[rows, :], kv_scr[0, idx].astype(BF16),
                                   kv_scr[1, idx].astype(BF16)], axis=0)
            outs.append(jnp.dot(lhs_scr[rows, :], rhs, preferred_element_type=F32))
        return outs

    def norm_gate(g, outs):
        goff = pl.multiple_of(g * grp_rows, grp_rows)
        for t in range(grp):
            rows = pl.ds(goff + t * tile, tile)
            on = _rms(outs[t], og_ref[...])
            rr = r_ref[rows, :].astype(F32)
            out_ref[rows, :] = (on * (rr * jax.nn.sigmoid(rr))).astype(BF16)

    run_two_stage(out_dots, norm_gate)


def _gla_scan(q, k, v, r, lg_hi, lg_lo, og, cum, batch, seq):
    t = batch * seq
    n_tiles = seq // GLA_TILE
    bh = lambda b, h: (b, h)
    per_head = lambda a: pl.BlockSpec((None, None, seq, a.shape[3]), lambda b, h: (b, h, 0, 0))
    return pl.pallas_call(
        _gla_kernel,
        grid=(batch, GLA_HEADS),
        in_specs=[
            per_head(q), per_head(k), per_head(v), per_head(r), per_head(lg_hi), per_head(lg_lo),
            _const_spec((1, GLA_DV)),
            _const_spec(cum.shape),
        ],
        out_specs=pl.BlockSpec((seq, GLA_DV), bh),
        out_shape=jax.ShapeDtypeStruct((t, GLA_V), BF16),
        scratch_shapes=[
            pltpu.VMEM((seq, GLA_TILE + 2 * GLA_DK), BF16),
            pltpu.VMEM((2, n_tiles, GLA_DK, GLA_DV), F32),
            pltpu.VMEM((2, n_tiles, GLA_DK, LANES), F32),
            pltpu.VMEM((GLA_GROUP * GLA_TILE, 2 * GLA_DK), F32),
        ],
        compiler_params=pltpu.CompilerParams(
            dimension_semantics=("arbitrary", "arbitrary"),
            vmem_limit_bytes=VMEM_LIMIT_BYTES),
        name="gla_scan",
    )(q, k, v, r, lg_hi, lg_lo, og, cum)


def _rope_tables(seq):
    f32 = np.float32
    pos = np.arange(seq)
    half = HEAD_DIM // 2
    inv = (f32(ROPE_THETA) ** (-np.arange(0, half, 2, dtype=f32) / f32(half))).astype(f32)
    ang_r = (pos // GRID_W).astype(f32)[:, None] * inv[None, :]
    ang_c = (pos % GRID_W).astype(f32)[:, None] * inv[None, :]
    cos_h = np.concatenate([np.cos(ang_r)] * 2 + [np.cos(ang_c)] * 2, axis=-1)
    sin_h = np.concatenate([-np.sin(ang_r), np.sin(ang_r), -np.sin(ang_c), np.sin(ang_c)], axis=-1)
    return (jnp.asarray(np.tile(cos_h, (1, 2)), dtype=F32),
            jnp.asarray(np.tile(sin_h, (1, 2)), dtype=F32))


def _gla_cumsum_matrix():
    n = GLA_TILE
    r = np.arange(n)[:, None]
    c = np.arange(n)[None, :]
    same = (r // GLA_CHUNK) == (c // GLA_CHUNK)
    prefix = (same & (c <= r)).astype(np.float32)
    suffix = (same & (c >= r)).astype(np.float32)
    return jnp.asarray(np.concatenate([prefix, prefix, suffix, suffix], axis=1), dtype=BF16)


def kernel(x, norm_mix, norm_mlp, attn_w_qkv, attn_q_norm, attn_k_norm, attn_w_o,
           gla_w_in, gla_w_gate_up, gla_b_gate, gla_out_norm, gla_w_o,
           mlp_w_in, mlp_w_out, final_norm):
    batch, seq, d = x.shape
    t = batch * seq
    h = x.reshape(t, d)
    row = lambda a: a.reshape(1, -1).astype(F32)

    cos, sin = _rope_tables(seq)
    gidx = np.arange(2 * LANES)
    same_head = gidx[:, None] // HEAD_DIM == gidx[None, :] // HEAD_DIM
    gsum = jnp.asarray(np.where(same_head, 1.0 / HEAD_DIM, 0.0), dtype=BF16)
    q_gain = jnp.tile(attn_q_norm[0], 2).reshape(1, LANES) * (LOG2E / math.sqrt(HEAD_DIM))
    k_gain = jnp.tile(attn_k_norm[0], 2).reshape(1, LANES)
    q, kt, v = _qkv_proj(h, row(norm_mix[0]), attn_w_qkv[0].astype(BF16), q_gain, k_gain,
                         cos, sin, gsum, batch, seq)
    n_layers = mlp_w_in.shape[0]
    o, (w1_all, w2_all, w_attn_o, w_gla_in, w_gla_o) = _attention(
        q, kt, v, batch, seq,
        [mlp_w_in.reshape(n_layers * d, D_FF), mlp_w_out.reshape(n_layers * D_FF, d),
         attn_w_o[0], gla_w_in[0], gla_w_o[0]])
    w1_all = w1_all.reshape(n_layers, d, D_FF)
    w2_all = w2_all.reshape(n_layers, D_FF, d)
    h = _oproj_mlp(h, o, w_attn_o, row(norm_mlp[0]),
                   w1_all, w2_all, 0, row(final_norm), False)

    n_codes = 2 * GATE_RANK
    w_codes = gla_w_in[0][:, -n_codes:]
    w_z = jnp.pad(jnp.tile(w_codes, (1, 3)), ((0, 0), (0, LANES - 3 * n_codes))).astype(BF16)
    wup = jnp.zeros((2, n_codes, GLA_QK), F32)
    wup = wup.at[0, :GATE_RANK].set(gla_w_gate_up[0, 0])
    wup = wup.at[1, GATE_RANK:].set(gla_w_gate_up[0, 1])
    wup = wup.reshape(2, n_codes, GLA_HEADS, GLA_DK).transpose(1, 2, 0, 3).reshape(n_codes, 2 * GLA_QK)
    wup_hi, wup_lo = _hi_lo(wup)
    w_up = jnp.concatenate(
        [wup_hi, wup_hi, wup_lo, jnp.zeros((LANES - 3 * n_codes, 2 * GLA_QK), BF16)], axis=0)
    b_gate = gla_b_gate[0].reshape(2, GLA_HEADS, GLA_DK).transpose(1, 0, 2).reshape(1, 2 * GLA_QK)
    gq, gk, gv, gr, lg_hi, lg_lo = _gla_inproj(h, row(norm_mix[1]), w_gla_in, w_z, w_up, b_gate,
                                               batch, seq)
    go = _gla_scan(gq, gk, gv, gr, lg_hi, lg_lo,
                   row(gla_out_norm[0]), _gla_cumsum_matrix(), batch, seq)
    h = _oproj_mlp(h, go, w_gla_o, row(norm_mlp[1]),
                   w1_all, w2_all, 1, row(final_norm), True)
    return h.reshape(batch, seq, d)
```
